```python
import jax, jax.numpy as jnp
from jax import lax
import numpy as np

D_MODEL = 1024
BATCH = 8
SEQ = 4096
DEPTH = 2

D_MIX = 2 * D_MODEL
ATTN_HEAD_DIM = 128
N_ATTN_HEADS = D_MODEL // ATTN_HEAD_DIM
D_ATTN = N_ATTN_HEADS * ATTN_HEAD_DIM
MOBA_BLOCK = 256
MOBA_TOPK = 3
Q_BLOCK = 128
SSD_HEAD_DIM = 64
D_SSD = D_MIX - D_ATTN
N_SSD_HEADS = D_SSD // SSD_HEAD_DIM
SSD_GROUPS = 2
SSD_STATE = 128
SSD_CONV = 4
SSD_CHUNK = 256
D_CONV_CH = D_SSD + 2 * SSD_GROUPS * SSD_STATE
D_IN_PROJ = 4 * D_ATTN + D_SSD + D_CONV_CH + N_SSD_HEADS
SEQ_ALIGN = 256
EPS = 1e-6

kernel_name = "hymba_moba_ssd_hybrid"


def rms_norm(x, w):
    xf = x.astype(jnp.float32)
    y = xf * lax.rsqrt(jnp.mean(xf * xf, axis=-1, keepdims=True) + EPS)
    return (y * w.astype(jnp.float32)).astype(x.dtype)


def alibi_slopes():
    h = jnp.arange(1, N_ATTN_HEADS + 1, dtype=jnp.float32)
    return jnp.exp2(-8.0 * h / N_ATTN_HEADS)


def moba_attention(q, k, v):
    b, s, h, dh = q.shape
    nb = s // MOBA_BLOCK
    nqc = s // Q_BLOCK
    n_sel = min(MOBA_TOPK, nb)
    scale = dh ** -0.5
    slopes = alibi_slopes()
    qh = q.transpose(0, 2, 1, 3)
    kblk = k.transpose(0, 2, 1, 3).reshape(b, h, nb, MOBA_BLOCK, dh)
    vblk = v.transpose(0, 2, 1, 3).reshape(b, h, nb, MOBA_BLOCK, dh)
    k_mean = jnp.mean(kblk.astype(jnp.float32), axis=3)
    gate = jnp.einsum('bhsd,bhnd->bhsn', qh.astype(jnp.float32), k_mean)
    q_blk = jnp.arange(s) // MOBA_BLOCK
    past = jnp.arange(nb)[None, :] < q_blk[:, None]
    gate = jnp.where(past, gate, -jnp.inf)
    _, sel = lax.top_k(gate, n_sel)
    q_c = qh.reshape(b, h, nqc, Q_BLOCK, dh).transpose(0, 2, 1, 3, 4).reshape(b * nqc, h, Q_BLOCK, dh)
    sel_c = sel.reshape(b, h, nqc, Q_BLOCK, n_sel).transpose(0, 2, 1, 3, 4).reshape(b * nqc, h, Q_BLOCK, n_sel)
    b_id = jnp.repeat(jnp.arange(b), nqc)
    c_id = jnp.tile(jnp.arange(nqc), b)
    head_ix = jnp.arange(h)[:, None, None]
    key_off = jnp.arange(MOBA_BLOCK)

    def one_block(args):
        qc, selc, bi, ci = args
        kb = lax.dynamic_index_in_dim(kblk, bi, 0, keepdims=False)
        vb = lax.dynamic_index_in_dim(vblk, bi, 0, keepdims=False)
        t = ci * Q_BLOCK + jnp.arange(Q_BLOCK)
        own = (ci * Q_BLOCK) // MOBA_BLOCK
        k_sel = kb[head_ix, selc]
        v_sel = vb[head_ix, selc]
        s_sel = jnp.einsum('hqd,hqkcd->hqkc', qc, k_sel, preferred_element_type=jnp.float32) * scale
        pos_sel = selc[..., None] * MOBA_BLOCK + key_off
        s_sel = s_sel - slopes[:, None, None, None] * (t[None, :, None, None] - pos_sel).astype(jnp.float32)
        valid = jnp.arange(n_sel)[None, :] < (t // MOBA_BLOCK)[:, None]
        s_sel = jnp.where(valid[None, :, :, None], s_sel, -jnp.inf)
        k_own = lax.dynamic_index_in_dim(kb, own, 1, keepdims=False)
        v_own = lax.dynamic_index_in_dim(vb, own, 1, keepdims=False)
        s_own = jnp.einsum('hqd,hcd->hqc', qc, k_own, preferred_element_type=jnp.float32) * scale
        dist = t[:, None] - (own * MOBA_BLOCK + key_off)[None, :]
        s_own = s_own - slopes[:, None, None] * dist.astype(jnp.float32)[None]
        s_own = jnp.where((dist >= 0)[None], s_own, -jnp.inf)
        scores = jnp.concatenate([s_sel.reshape(h, Q_BLOCK, n_sel * MOBA_BLOCK), s_own], axis=-1)
        p = jax.nn.softmax(scores, axis=-1).astype(v.dtype)
        p_sel = p[..., :n_sel * MOBA_BLOCK].reshape(h, Q_BLOCK, n_sel, MOBA_BLOCK)
        p_own = p[..., n_sel * MOBA_BLOCK:]
        return (jnp.einsum('hqkc,hqkcd->hqd', p_sel, v_sel)
                + jnp.einsum('hqc,hcd->hqd', p_own, v_own))

    out = lax.map(one_block, (q_c, sel_c, b_id, c_id))
    return out.reshape(b, nqc, h, Q_BLOCK, dh).transpose(0, 1, 3, 2, 4).reshape(b, s, h * dh)


def causal_depthwise_conv(u, w, bias):
    out = lax.conv_general_dilated(u, w[:, None, :], window_strides=(1,),
                                   padding=[(SSD_CONV - 1, 0)],
                                   dimension_numbers=('NWC', 'WIO', 'NWC'),
                                   feature_group_count=u.shape[-1])
    return out + bias


def ssd_scan(xs, bm, cm, dt, a_log, d_skip):
    b, s = xs.shape[:2]
    nc = s // SSD_CHUNK
    hg = N_SSD_HEADS // SSD_GROUPS
    a = -jnp.exp(a_log.astype(jnp.float32)).reshape(SSD_GROUPS, hg)
    x = xs.reshape(b, nc, SSD_CHUNK, SSD_GROUPS, hg, SSD_HEAD_DIM)
    bc = bm.reshape(b, nc, SSD_CHUNK, SSD_GROUPS, SSD_STATE)
    cc = cm.reshape(b, nc, SSD_CHUNK, SSD_GROUPS, SSD_STATE)
    dtc = dt.reshape(b, nc, SSD_CHUNK, SSD_GROUPS, hg)
    xdt = x * dtc[..., None]
    a_cs = jnp.cumsum((dtc * a).transpose(0, 1, 3, 4, 2), axis=-1)
    causal = jnp.tril(jnp.ones((SSD_CHUNK, SSD_CHUNK), dtype=bool))
    decay = jnp.exp(jnp.where(causal, a_cs[..., :, None] - a_cs[..., None, :], -jnp.inf))
    cb = jnp.einsum('bclgn,bcsgn->bcgls', cc, bc)
    y_diag = jnp.einsum('bcgls,bcghls,bcsghp->bclghp', cb, decay, xdt)
    decay_states = jnp.exp(a_cs[..., -1:] - a_cs)
    states = jnp.einsum('bclgn,bcghl,bclghp->bcghpn', bc, decay_states, xdt)
    chunk_decay = jnp.exp(a_cs[..., -1])

    def step(carry, inp):
        st, dec = inp
        return carry * dec[..., None, None] + st, carry

    init = jnp.zeros((b, SSD_GROUPS, hg, SSD_HEAD_DIM, SSD_STATE), jnp.float32)
    _, prev = lax.scan(step, init, (jnp.moveaxis(states, 1, 0), jnp.moveaxis(chunk_decay, 1, 0)))
    prev = jnp.moveaxis(prev, 0, 1)
    y_off = jnp.einsum('bclgn,bcghpn,bcghl->bclghp', cc, prev, jnp.exp(a_cs))
    y = y_diag + y_off + x * d_skip.astype(jnp.float32).reshape(SSD_GROUPS, hg)[:, :, None]
    return y.reshape(b, s, N_SSD_HEADS * SSD_HEAD_DIM)


def hybrid_layer(x, ln_w, w_in, q_norm_w, k_norm_w, conv_w, conv_b, dt_bias, a_log, d_skip, ssd_norm_w, w_out):
    b, s, _ = x.shape
    h = rms_norm(x, ln_w)
    proj = h @ w_in
    cuts = [D_ATTN, 2 * D_ATTN, 3 * D_ATTN, 4 * D_ATTN, 4 * D_ATTN + D_SSD, 4 * D_ATTN + D_SSD + D_CONV_CH]
    q, k, v, g_attn, z, xbc, dt_raw = jnp.split(proj, cuts, axis=-1)
    q = rms_norm(q.reshape(b, s, N_ATTN_HEADS, ATTN_HEAD_DIM), q_norm_w)
    k = rms_norm(k.reshape(b, s, N_ATTN_HEADS, ATTN_HEAD_DIM), k_norm_w)
    v = v.reshape(b, s, N_ATTN_HEADS, ATTN_HEAD_DIM)
    attn = moba_attention(q, k, v) * jax.nn.silu(g_attn)
    xbc = jax.nn.silu(causal_depthwise_conv(xbc, conv_w, conv_b)).astype(jnp.float32)
    xs, bm, cm = jnp.split(xbc, [D_SSD, D_SSD + SSD_GROUPS * SSD_STATE], axis=-1)
    dt = jax.nn.softplus(dt_raw.astype(jnp.float32) + dt_bias.astype(jnp.float32))
    y = ssd_scan(xs.reshape(b, s, N_SSD_HEADS, SSD_HEAD_DIM),
                 bm.reshape(b, s, SSD_GROUPS, SSD_STATE),
                 cm.reshape(b, s, SSD_GROUPS, SSD_STATE), dt, a_log, d_skip)
    y = y * jax.nn.silu(z.astype(jnp.float32))
    y = rms_norm(y.reshape(b, s, SSD_GROUPS, D_SSD // SSD_GROUPS),
                 ssd_norm_w.reshape(SSD_GROUPS, D_SSD // SSD_GROUPS)).reshape(b, s, D_SSD).astype(x.dtype)
    mixed = jnp.concatenate([attn, y], axis=-1)
    return x + mixed @ w_out


def setup_inputs(seed: int = 0) -> dict:
    key = jax.random.key(seed)
    ks = jax.random.split(key, 13)
    f32 = jnp.float32
    x = jax.random.normal(ks[0], (BATCH, SEQ, D_MODEL), f32)
    ln_w = 1.0 + 0.01 * jax.random.normal(ks[1], (DEPTH, D_MODEL), f32)
    w_in = jax.random.normal(ks[2], (DEPTH, D_MODEL, D_IN_PROJ), f32) * D_MODEL ** -0.5
    q_norm_w = 1.0 + 0.01 * jax.random.normal(ks[3], (DEPTH, ATTN_HEAD_DIM), f32)
    k_norm_w = 1.0 + 0.01 * jax.random.normal(ks[4], (DEPTH, ATTN_HEAD_DIM), f32)
    conv_w = jax.random.normal(ks[5], (DEPTH, SSD_CONV, D_CONV_CH), f32) * SSD_CONV ** -0.5
    conv_b = 0.01 * jax.random.normal(ks[6], (DEPTH, D_CONV_CH), f32)
    dt0 = jnp.exp(jax.random.uniform(ks[7], (DEPTH, N_SSD_HEADS), f32,
                                     minval=float(np.log(1e-3)), maxval=float(np.log(1e-1))))
    dt_bias = dt0 + jnp.log(-jnp.expm1(-dt0))
    a_log = jnp.log(jax.random.uniform(ks[8], (DEPTH, N_SSD_HEADS), f32, minval=1.0, maxval=16.0))
    d_skip = 1.0 + 0.01 * jax.random.normal(ks[9], (DEPTH, N_SSD_HEADS), f32)
    ssd_norm_w = 1.0 + 0.01 * jax.random.normal(ks[10], (DEPTH, D_SSD), f32)
    w_out = jax.random.normal(ks[11], (DEPTH, D_MIX, D_MODEL), f32) * D_MIX ** -0.5
    return {"x": x, "ln_w": ln_w, "w_in": w_in, "q_norm_w": q_norm_w, "k_norm_w": k_norm_w,
            "conv_w": conv_w, "conv_b": conv_b, "dt_bias": dt_bias, "a_log": a_log,
            "d_skip": d_skip, "ssd_norm_w": ssd_norm_w, "w_out": w_out}


def reference(x, ln_w, w_in, q_norm_w, k_norm_w, conv_w, conv_b, dt_bias, a_log, d_skip, ssd_norm_w, w_out):
    s = x.shape[1]
    pad = (-s) % SEQ_ALIGN
    h = jnp.pad(x, ((0, 0), (0, pad), (0, 0)))
    for i in range(DEPTH):
        h = hybrid_layer(h, ln_w[i], w_in[i], q_norm_w[i], k_norm_w[i], conv_w[i], conv_b[i],
                         dt_bias[i], a_log[i], d_skip[i], ssd_norm_w[i], w_out[i])
    return h[:, :s]
```

```python
import functools

import jax
import jax.numpy as jnp
import numpy as np
from jax import lax
from jax.experimental import pallas as pl
from jax.experimental.pallas import tpu as pltpu

F32 = jnp.float32
BF16 = jnp.bfloat16

HEAD_DIM = 128
N_HEADS = 8
D_ATTN = N_HEADS * HEAD_DIM
BLOCK = 256
TOPK = 3
SSD_HEAD = 64
SSD_STATE = 128
SSD_GROUPS = 2
CONV_W = 4
EPS = 1e-6
NEG = -1e30
LANES = 128
VMEM_LIMIT = 56 * 1024 * 1024


def _sigmoid(x):
    return 1.0 / (1.0 + jnp.exp(-x))


def _dot_nt(a, b):
    return lax.dot_general(a, b, (((1,), (1,)), ((), ())), preferred_element_type=F32)


def _dot(a, b):
    return jnp.dot(a, b, preferred_element_type=F32)


def _split2(x):
    hi = x.astype(BF16)
    lo = (x - hi.astype(F32)).astype(BF16)
    return hi, lo


def _inproj_attn_kernel(x_ref, lnw_ref, w_ref, wvt_ref, qw_ref, kw_ref, o_ref, vt_ref, km_ref, h_scr, *, tm):
    j = pl.program_id(1)
    d = D_ATTN
    nblk = tm // BLOCK
    scale = HEAD_DIM ** -0.5

    @pl.when(j == 0)
    def _():
        x = x_ref[...]
        ms = jnp.mean(x * x, axis=-1, keepdims=True)
        h_scr[...] = (x * lax.rsqrt(ms + EPS) * lnw_ref[...]).astype(BF16)

    def head_norm(nw_ref, mult, want_mean):
        for c in range(d // 256):
            acc = _dot(h_scr[...], w_ref[:, c * 256:(c + 1) * 256])
            for hh in range(2):
                lo = c * 256 + hh * HEAD_DIM
                a = acc[:, hh * HEAD_DIM:(hh + 1) * HEAD_DIM]
                y = a * lax.rsqrt(jnp.mean(a * a, axis=-1, keepdims=True) + EPS) * nw_ref[...]
                if mult is not None:
                    y = y * mult
                o_ref[:, lo:lo + HEAD_DIM] = y.astype(BF16)
                if want_mean:
                    for b in range(nblk):
                        km_ref[0, b:b + 1, lo:lo + HEAD_DIM] = jnp.mean(
                            y[b * BLOCK:(b + 1) * BLOCK], axis=0, keepdims=True)

    @pl.when(j == 0)
    def _():
        head_norm(qw_ref, scale, False)

    @pl.when(j == 1)
    def _():
        head_norm(kw_ref, None, True)

    @pl.when(j == 2)
    def _():
        for c in range(d // 256):
            acc = _dot(h_scr[...], w_ref[:, c * 256:(c + 1) * 256])
            o_ref[:, c * 256:(c + 1) * 256] = (acc * _sigmoid(acc)).astype(BF16)

    @pl.when(j == 3)
    def _():
        for b in range(nblk):
            vt_ref[b] = _dot_nt(wvt_ref[...], h_scr[b * BLOCK:(b + 1) * BLOCK, :]).astype(BF16)


def _inproj_attn(x2, ln_w, w_qkg, w_vt, qw, kw, *, tm):
    m, d = x2.shape
    nblk = tm // BLOCK
    grid = (m // tm, 4)
    return pl.pallas_call(
        functools.partial(_inproj_attn_kernel, tm=tm),
        grid=grid,
        in_specs=[
            pl.BlockSpec((tm, d), lambda i, j: (i, 0)),
            pl.BlockSpec((1, d), lambda i, j: (0, 0)),
            pl.BlockSpec((d, D_ATTN), lambda i, j: (0, jnp.minimum(j, 2))),
            pl.BlockSpec((D_ATTN, d), lambda i, j: (0, 0)),
            pl.BlockSpec((1, HEAD_DIM), lambda i, j: (0, 0)),
            pl.BlockSpec((1, HEAD_DIM), lambda i, j: (0, 0)),
        ],
        out_specs=[
            pl.BlockSpec((tm, D_ATTN), lambda i, j: (i, jnp.minimum(j, 2))),
            pl.BlockSpec((nblk, D_ATTN, BLOCK), lambda i, j: (i, 0, 0)),
            pl.BlockSpec((1, nblk, D_ATTN), lambda i, j: (i, 0, 0)),
        ],
        out_shape=[
            jax.ShapeDtypeStruct((m, 3 * D_ATTN), BF16),
            jax.ShapeDtypeStruct((m // BLOCK, D_ATTN, BLOCK), BF16),
            jax.ShapeDtypeStruct((m // tm, nblk, D_ATTN), F32),
        ],
        scratch_shapes=[pltpu.VMEM((tm, d), BF16)],
        compiler_params=pltpu.CompilerParams(
            dimension_semantics=("arbitrary", "arbitrary"), vmem_limit_bytes=VMEM_LIMIT),
        name="inproj_attn",
    )(x2, ln_w, w_qkg, w_vt, qw, kw)


def _inproj_ssd_kernel(x_ref, lnw_ref, w_ref, o_ref):
    x = x_ref[...]
    ms = jnp.mean(x * x, axis=-1, keepdims=True)
    h = (x * lax.rsqrt(ms + EPS) * lnw_ref[...]).astype(BF16)
    n = w_ref.shape[1]
    for c in range(n // LANES // 3):
        sl = slice(c * 3 * LANES, (c + 1) * 3 * LANES)
        o_ref[:, sl] = _dot(h, w_ref[:, sl])


def _inproj_ssd(x2, ln_w, w_ssd, *, tm):
    m, d = x2.shape
    n = w_ssd.shape[1]
    return pl.pallas_call(
        _inproj_ssd_kernel,
        grid=(m // tm,),
        in_specs=[
            pl.BlockSpec((tm, d), lambda i: (i, 0)),
            pl.BlockSpec((1, d), lambda i: (0, 0)),
            pl.BlockSpec((d, n), lambda i: (0, 0)),
        ],
        out_specs=pl.BlockSpec((tm, n), lambda i: (i, 0)),
        out_shape=jax.ShapeDtypeStruct((m, n), F32),
        compiler_params=pltpu.CompilerParams(
            dimension_semantics=("arbitrary",), vmem_limit_bytes=VMEM_LIMIT),
        name="inproj_ssd",
    )(x2, ln_w, w_ssd)


def _attn_kernel(q_ref, k_ref, vt_ref, g_ref, km_ref, amat_ref, srow_ref, o_ref, off_scr, *, nb):
    i = pl.program_id(2)
    q = q_ref[...]
    km_hi, km_lo = _split2(km_ref[...])
    gate = _dot_nt(km_hi, q) + _dot_nt(km_lo, q)
    blk = lax.broadcasted_iota(jnp.int32, (nb, BLOCK), 0)
    elig = blk < i
    gate = jnp.where(elig, gate, -jnp.inf)
    rank = jnp.zeros((nb, BLOCK), F32)
    for m in range(nb):
        gm = gate[m:m + 1, :]
        beats = (gm > gate) | ((gm == gate) & (blk > m))
        rank = rank + beats.astype(F32)
    sel = elig & (rank < TOPK)
    srow = srow_ref[0]
    off_scr[...] = srow * ((blk - i) * BLOCK).astype(F32) + jnp.where(sel, 0.0, NEG)

    s = _dot_nt(k_ref[i], q) + amat_ref[0, 1]
    m0 = jnp.max(s, axis=0, keepdims=True)
    p = jnp.exp(s - m0)
    l0 = jnp.sum(p, axis=0, keepdims=True)
    acc0 = _dot(vt_ref[i], p.astype(BF16))

    def body(n, carry):
        m_run, l_run, acc = carry
        t = _dot_nt(k_ref[n], q) + amat_ref[0, 0]
        off = off_scr[pl.ds(n, 1), :]
        m_new = jnp.maximum(m_run, jnp.max(t, axis=0, keepdims=True) + off)
        alpha = jnp.exp(m_run - m_new)
        p = jnp.exp(t - (m_new - off))
        l_new = alpha * l_run + jnp.sum(p, axis=0, keepdims=True)
        acc = alpha * acc + _dot(vt_ref[n], p.astype(BF16))
        return m_new, l_new, acc

    _, l_fin, acc = lax.fori_loop(0, i, body, (m0, l0, acc0))
    out = (acc / l_fin).T
    o_ref[...] = (out * g_ref[...].astype(F32)).astype(BF16)


def _moba_attn(qkg, vt, kmean, amat, srow, *, batch, seq):
    nb = seq // BLOCK
    m = batch * seq
    qkg3 = qkg.reshape(m // BLOCK, BLOCK, 3 * D_ATTN)
    km2 = kmean.reshape(m // BLOCK, D_ATTN)
    return pl.pallas_call(
        functools.partial(_attn_kernel, nb=nb),
        grid=(batch, N_HEADS, nb),
        in_specs=[
            pl.BlockSpec((None, BLOCK, HEAD_DIM), lambda b, h, i: (b * nb + i, 0, h)),
            pl.BlockSpec((nb, BLOCK, HEAD_DIM), lambda b, h, i: (b, 0, N_HEADS + h)),
            pl.BlockSpec((nb, HEAD_DIM, BLOCK), lambda b, h, i: (b, h, 0)),
            pl.BlockSpec((None, BLOCK, HEAD_DIM), lambda b, h, i: (b * nb + i, 0, 2 * N_HEADS + h)),
            pl.BlockSpec((nb, HEAD_DIM), lambda b, h, i: (b, h)),
            pl.BlockSpec((1, 2, BLOCK, BLOCK), lambda b, h, i: (h, 0, 0, 0)),
            pl.BlockSpec((1, 1, BLOCK), lambda b, h, i: (h, 0, 0)),
        ],
        out_specs=pl.BlockSpec((None, BLOCK, HEAD_DIM), lambda b, h, i: (b * nb + i, 0, h)),
        out_shape=jax.ShapeDtypeStruct((m // BLOCK, BLOCK, D_ATTN), BF16),
        scratch_shapes=[pltpu.VMEM((nb, BLOCK), F32)],
        compiler_params=pltpu.CompilerParams(
            dimension_semantics=("arbitrary", "arbitrary", "arbitrary"), vmem_limit_bytes=VMEM_LIMIT),
        name="moba_attn",
    )(qkg3, qkg3, vt, qkg3, km2, amat, srow).reshape(m, D_ATTN)


def _alibi_tables():
    slopes = np.exp2(-8.0 * np.arange(1, N_HEADS + 1, dtype=np.float64) / N_HEADS).astype(np.float32)
    c = np.arange(BLOCK, dtype=np.float32)
    past = slopes[:, None, None] * np.broadcast_to(c[:, None], (BLOCK, BLOCK))[None]
    causal = np.where(c[:, None] <= c[None, :], 0.0, NEG).astype(np.float32)
    amat = np.stack([past, past + causal[None]], axis=1).astype(np.float32)
    srow = np.broadcast_to(slopes[:, None, None], (N_HEADS, 1, BLOCK)).astype(np.float32)
    return jnp.asarray(amat), jnp.asarray(srow)


def _ssd_kernel(p_ref, cw_ref, cb_ref, dtb_ref, alog_ref, dsk_ref, nw_ref, tri_ref, ex_ref,
                o_ref, xpad_scr, st_scr, y_scr, *, d_ssd):
    c = pl.program_id(1)
    d_conv = d_ssd + 2 * SSD_GROUPS * SSD_STATE
    gw = d_ssd // SSD_GROUPS

    @pl.when(c == 0)
    def _():
        xpad_scr[0:8, :] = jnp.zeros((8, d_conv), F32)
        st_scr[...] = jnp.zeros_like(st_scr)

    z = p_ref[:, 0:d_ssd]
    u = p_ref[:, d_ssd:d_ssd + d_conv]
    dtr = p_ref[:, d_ssd + d_conv:d_ssd + d_conv + LANES]

    xpad_scr[8:8 + BLOCK, :] = u
    conv = cb_ref[...] + cw_ref[CONV_W - 1:CONV_W, :] * u
    for s in range(1, CONV_W):
        conv = conv + cw_ref[CONV_W - 1 - s:CONV_W - s, :] * xpad_scr[8 - s:8 - s + BLOCK, :]
    xpad_scr[0:8, :] = u[BLOCK - 8:BLOCK, :]
    act = conv * _sigmoid(conv)
    xs = act[:, 0:d_ssd]
    bm = act[:, d_ssd:d_ssd + SSD_GROUPS * SSD_STATE]
    cm = act[:, d_ssd + SSD_GROUPS * SSD_STATE:d_conv]

    v = dtr + dtb_ref[...]
    dt = jnp.maximum(v, 0.0) + jnp.log1p(jnp.exp(-jnp.abs(v)))
    da = dt * (-jnp.exp(alog_ref[...]))
    hi = da.astype(BF16)
    r1 = da - hi.astype(F32)
    mid = r1.astype(BF16)
    lo = (r1 - mid.astype(F32)).astype(BF16)
    tri = tri_ref[...]
    acs = _dot(tri, hi) + _dot(tri, mid) + _dot(tri, lo)
    acs_t = acs.T
    e1 = jnp.exp(acs)
    e2 = jnp.exp(acs[BLOCK - 1:BLOCK, :] - acs)

    def expand(a):
        a_hi, a_lo = _split2(a)
        return _dot(jnp.concatenate([a_hi, a_lo], axis=1), ex_ref[...])

    dtx = expand(dt)
    e1x = expand(e1)
    w2x = expand(dt * e2)
    xdt_b = (xs * dtx).astype(BF16)

    row = lax.broadcasted_iota(jnp.int32, (BLOCK, BLOCK), 0)
    col = lax.broadcasted_iota(jnp.int32, (BLOCK, BLOCK), 1)
    causal = row >= col
    lane = lax.broadcasted_iota(jnp.int32, (BLOCK, LANES), 1)
    heads_per_group = gw // SSD_HEAD
    for g in range(SSD_GROUPS):
        bg = bm[:, g * SSD_STATE:(g + 1) * SSD_STATE]
        cg_b = cm[:, g * SSD_STATE:(g + 1) * SSD_STATE].astype(BF16)
        gmat = _dot_nt(cg_b, bg.astype(BF16))
        for pr in range(heads_per_group // 2):
            slab = slice(g * gw + pr * LANES, g * gw + (pr + 1) * LANES)
            xp = xdt_b[:, slab]
            pair = None
            for hh in range(2):
                h = g * heads_per_group + pr * 2 + hh
                diff = acs[:, h:h + 1] - acs_t[h:h + 1, :]
                w = (gmat * jnp.exp(jnp.where(causal, diff, -jnp.inf))).astype(BF16)
                keep = (lane < SSD_HEAD) if hh == 0 else (lane >= SSD_HEAD)
                part = _dot(w, jnp.where(keep, xp, jnp.zeros_like(xp)))
                pair = part if pair is None else pair + part
            y_scr[:, slab] = pair
        gsl = slice(g * gw, (g + 1) * gw)
        st = st_scr[g]
        y_scr[:, gsl] = y_scr[:, gsl] + _dot(cg_b, st.astype(BF16)) * e1x[:, gsl]
        xw = (xs[:, gsl] * w2x[:, gsl]).astype(BF16)
        st_scr[g] = st * e1x[BLOCK - 1:BLOCK, gsl] + _dot(bg.T.astype(BF16), xw)

    y = y_scr[...] + xs * dsk_ref[...]
    y = y * (z * _sigmoid(z))
    for g in range(SSD_GROUPS):
        gsl = slice(g * gw, (g + 1) * gw)
        yg = y[:, gsl]
        ms = jnp.mean(yg * yg, axis=-1, keepdims=True)
        o_ref[:, gsl] = (yg * lax.rsqrt(ms + EPS) * nw_ref[:, gsl]).astype(BF16)


def _ssd(proj, cw, cb, dtb, alog, dsk, nw, tri, ex, *, batch, seq, d_ssd):
    nc = seq // BLOCK
    m, n = proj.shape
    d_conv = d_ssd + 2 * SSD_GROUPS * SSD_STATE
    gw = d_ssd // SSD_GROUPS
    const = lambda b, c: (0, 0)
    return pl.pallas_call(
        functools.partial(_ssd_kernel, d_ssd=d_ssd),
        grid=(batch, nc),
        in_specs=[
            pl.BlockSpec((BLOCK, n), lambda b, c: (b * nc + c, 0)),
            pl.BlockSpec((CONV_W, d_conv), const),
            pl.BlockSpec((1, d_conv), const),
            pl.BlockSpec((1, LANES), const),
            pl.BlockSpec((1, LANES), const),
            pl.BlockSpec((1, d_ssd), const),
            pl.BlockSpec((1, d_ssd), const),
            pl.BlockSpec((BLOCK, BLOCK), const),
            pl.BlockSpec((2 * LANES, d_ssd), const),
        ],
        out_specs=pl.BlockSpec((BLOCK, d_ssd), lambda b, c: (b * nc + c, 0)),
        out_shape=jax.ShapeDtypeStruct((m, d_ssd), BF16),
        scratch_shapes=[
            pltpu.VMEM((8 + BLOCK, d_conv), F32),
            pltpu.VMEM((SSD_GROUPS, SSD_STATE, gw), F32),
            pltpu.VMEM((BLOCK, d_ssd), F32),
        ],
        compiler_params=pltpu.CompilerParams(
            dimension_semantics=("arbitrary", "arbitrary"), vmem_limit_bytes=VMEM_LIMIT),
        name="ssd",
    )(proj, cw, cb, dtb, alog, dsk, nw, tri, ex)


def _outproj_kernel(a_ref, y_ref, x_ref, w_ref, o_ref):
    da = a_ref.shape[1]
    o_ref[...] = x_ref[...] + _dot(a_ref[...], w_ref[0:da, :]) + _dot(y_ref[...], w_ref[da:, :])


def _outproj(attn, y, x2, w_out, *, tm):
    m, d = x2.shape
    return pl.pallas_call(
        _outproj_kernel,
        grid=(m // tm,),
        in_specs=[
            pl.BlockSpec((tm, attn.shape[1]), lambda i: (i, 0)),
            pl.BlockSpec((tm, y.shape[1]), lambda i: (i, 0)),
            pl.BlockSpec((tm, d), lambda i: (i, 0)),
            pl.BlockSpec(w_out.shape, lambda i: (0, 0)),
        ],
        out_specs=pl.BlockSpec((tm, d), lambda i: (i, 0)),
        out_shape=jax.ShapeDtypeStruct((m, d), F32),
        compiler_params=pltpu.CompilerParams(
            dimension_semantics=("arbitrary",), vmem_limit_bytes=VMEM_LIMIT),
        name="outproj",
    )(attn, y, x2, w_out)


def _row_tile(m, want):
    t = want
    while m % t:
        t //= 2
    return t


def _layer(x2, ln_w, w_in, q_norm_w, k_norm_w, conv_w, conv_b, dt_bias, a_log, d_skip, ssd_norm_w, w_out,
           tables, *, batch, seq):
    d = x2.shape[1]
    d_ssd = w_out.shape[0] - D_ATTN
    n_ssd_heads = d_ssd // SSD_HEAD
    d_conv = d_ssd + 2 * SSD_GROUPS * SSD_STATE
    amat, srow, tri, ex = tables
    m = x2.shape[0]

    wq, wk, wv, wg = (w_in[:, k * D_ATTN:(k + 1) * D_ATTN] for k in range(4))
    w_qkg = jnp.concatenate([wq, wk, wg], axis=1).astype(BF16)
    w_vt = wv.T.astype(BF16)
    w_rest = w_in[:, 4 * D_ATTN:]
    pad = (-w_rest.shape[1]) % (3 * LANES)
    w_ssd = jnp.pad(w_rest, ((0, 0), (0, pad))).astype(BF16)
    lnw = ln_w.reshape(1, d)

    def lane_pad(v):
        return jnp.pad(v, (0, LANES - v.shape[0])).reshape(1, LANES)

    qkg, vt, kmean = _inproj_attn(x2, lnw, w_qkg, w_vt, q_norm_w.reshape(1, HEAD_DIM),
                                  k_norm_w.reshape(1, HEAD_DIM), tm=_row_tile(m, 1024))
    attn = _moba_attn(qkg, vt, kmean, amat, srow, batch=batch, seq=seq)
    proj = _inproj_ssd(x2, lnw, w_ssd, tm=_row_tile(m, 512))
    y = _ssd(proj, conv_w, conv_b.reshape(1, d_conv), lane_pad(dt_bias), lane_pad(a_log),
             jnp.repeat(d_skip, SSD_HEAD).reshape(1, d_ssd), ssd_norm_w.reshape(1, d_ssd), tri, ex,
             batch=batch, seq=seq, d_ssd=d_ssd)
    del n_ssd_heads
    return _outproj(attn, y, x2, w_out.astype(BF16), tm=_row_tile(m, 512))


def kernel(x, ln_w, w_in, q_norm_w, k_norm_w, conv_w, conv_b, dt_bias, a_log, d_skip, ssd_norm_w, w_out):
    batch, seq, d = x.shape
    assert seq % BLOCK == 0 and d == D_ATTN
    d_ssd = w_out.shape[1] - D_ATTN
    n_heads = d_ssd // SSD_HEAD
    amat, srow = _alibi_tables()
    tri = jnp.asarray(np.tril(np.ones((BLOCK, BLOCK), np.float32))).astype(BF16)
    ex_np = np.zeros((LANES, d_ssd), np.float32)
    ex_np[np.repeat(np.arange(n_heads), SSD_HEAD), np.arange(d_ssd)] = 1.0
    ex = jnp.asarray(np.concatenate([ex_np, ex_np], axis=0)).astype(BF16)
    tables = (amat, srow, tri, ex)
    h = x.reshape(batch * seq, d)
    for i in range(ln_w.shape[0]):
        h = _layer(h, ln_w[i], w_in[i], q_norm_w[i], k_norm_w[i], conv_w[i], conv_b[i], dt_bias[i],
                   a_log[i], d_skip[i], ssd_norm_w[i], w_out[i], tables, batch=batch, seq=seq)
    return h.reshape(batch, seq, d)
```

```python
import functools

import jax
import jax.numpy as jnp
import numpy as np
from jax import lax
from jax.experimental import pallas as pl
from jax.experimental.pallas import tpu as pltpu

F32 = jnp.float32
BF16 = jnp.bfloat16

HEAD_DIM = 128
N_HEADS = 8
D_ATTN = N_HEADS * HEAD_DIM
BLOCK = 256
TOPK = 3
SSD_HEAD = 64
SSD_STATE = 128
SSD_GROUPS = 2
CONV_W = 4
EPS = 1e-6
NEG = -1e30
LOG2E = 1.4426950408889634
ATTN_HEADS_PER_STEP = 4
LANES = 128
VMEM_LIMIT = 56 * 1024 * 1024


def _sigmoid(x):
    return 1.0 / (1.0 + jnp.exp(-x))


def _dot_nt(a, b):
    return lax.dot_general(a, b, (((1,), (1,)), ((), ())), preferred_element_type=F32)


def _dot(a, b):
    return jnp.dot(a, b, preferred_element_type=F32)


def _split2(x):
    hi = x.astype(BF16)
    lo = (x - hi.astype(F32)).astype(BF16)
    return hi, lo


def _inproj_attn_kernel(x_ref, lnw_ref, w_ref, wvt_ref, qw_ref, kw_ref, o_ref, vt_ref, km_ref, h_scr, *, tm):
    j = pl.program_id(1)
    d = D_ATTN
    nblk = tm // BLOCK
    scale = HEAD_DIM ** -0.5 * LOG2E

    @pl.when(j == 0)
    def _():
        x = x_ref[...]
        ms = jnp.mean(x * x, axis=-1, keepdims=True)
        h_scr[...] = (x * lax.rsqrt(ms + EPS) * lnw_ref[...]).astype(BF16)

    def head_norm(nw_ref, mult, want_mean):
        for c in range(d // 256):
            acc = _dot(h_scr[...], w_ref[:, c * 256:(c + 1) * 256])
            for hh in range(2):
                lo = c * 256 + hh * HEAD_DIM
                a = acc[:, hh * HEAD_DIM:(hh + 1) * HEAD_DIM]
                y = a * lax.rsqrt(jnp.mean(a * a, axis=-1, keepdims=True) + EPS) * nw_ref[...]
                if mult is not None:
                    y = y * mult
                o_ref[:, lo:lo + HEAD_DIM] = y.astype(BF16)
                if want_mean:
                    for b in range(nblk):
                        km_ref[0, b:b + 1, lo:lo + HEAD_DIM] = jnp.mean(
                            y[b * BLOCK:(b + 1) * BLOCK], axis=0, keepdims=True)

    @pl.when(j == 0)
    def _():
        head_norm(qw_ref, scale, False)

    @pl.when(j == 1)
    def _():
        head_norm(kw_ref, None, True)

    @pl.when(j == 2)
    def _():
        for c in range(d // 256):
            acc = _dot(h_scr[...], w_ref[:, c * 256:(c + 1) * 256])
            o_ref[:, c * 256:(c + 1) * 256] = (acc * _sigmoid(acc)).astype(BF16)

    @pl.when(j == 3)
    def _():
        for b in range(nblk):
            vt_ref[b] = _dot_nt(wvt_ref[...], h_scr[b * BLOCK:(b + 1) * BLOCK, :]).astype(BF16)


def _inproj_attn(x2, ln_w, w_qkg, w_vt, qw, kw, *, tm):
    m, d = x2.shape
    nblk = tm // BLOCK
    grid = (m // tm, 4)
    return pl.pallas_call(
        functools.partial(_inproj_attn_kernel, tm=tm),
        grid=grid,
        in_specs=[
            pl.BlockSpec((tm, d), lambda i, j: (i, 0)),
            pl.BlockSpec((1, d), lambda i, j: (0, 0)),
            pl.BlockSpec((d, D_ATTN), lambda i, j: (0, jnp.minimum(j, 2))),
            pl.BlockSpec((D_ATTN, d), lambda i, j: (0, 0)),
            pl.BlockSpec((1, HEAD_DIM), lambda i, j: (0, 0)),
            pl.BlockSpec((1, HEAD_DIM), lambda i, j: (0, 0)),
        ],
        out_specs=[
            pl.BlockSpec((tm, D_ATTN), lambda i, j: (i, jnp.minimum(j, 2))),
            pl.BlockSpec((nblk, D_ATTN, BLOCK), lambda i, j: (i, 0, 0)),
            pl.BlockSpec((1, nblk, D_ATTN), lambda i, j: (i, 0, 0)),
        ],
        out_shape=[
            jax.ShapeDtypeStruct((m, 3 * D_ATTN), BF16),
            jax.ShapeDtypeStruct((m // BLOCK, D_ATTN, BLOCK), BF16),
            jax.ShapeDtypeStruct((m // tm, nblk, D_ATTN), F32),
        ],
        scratch_shapes=[pltpu.VMEM((tm, d), BF16)],
        compiler_params=pltpu.CompilerParams(
            dimension_semantics=("arbitrary", "arbitrary"), vmem_limit_bytes=VMEM_LIMIT),
        name="inproj_attn",
    )(x2, ln_w, w_qkg, w_vt, qw, kw)


def _inproj_ssd_kernel(x_ref, lnw_ref, w_ref, o_ref):
    x = x_ref[...]
    ms = jnp.mean(x * x, axis=-1, keepdims=True)
    h = (x * lax.rsqrt(ms + EPS) * lnw_ref[...]).astype(BF16)
    n = w_ref.shape[1]
    for c in range(n // LANES // 3):
        sl = slice(c * 3 * LANES, (c + 1) * 3 * LANES)
        o_ref[:, sl] = _dot(h, w_ref[:, sl])


def _inproj_ssd(x2, ln_w, w_ssd, *, tm):
    m, d = x2.shape
    n = w_ssd.shape[1]
    return pl.pallas_call(
        _inproj_ssd_kernel,
        grid=(m // tm,),
        in_specs=[
            pl.BlockSpec((tm, d), lambda i: (i, 0)),
            pl.BlockSpec((1, d), lambda i: (0, 0)),
            pl.BlockSpec((d, n), lambda i: (0, 0)),
        ],
        out_specs=pl.BlockSpec((tm, n), lambda i: (i, 0)),
        out_shape=jax.ShapeDtypeStruct((m, n), F32),
        compiler_params=pltpu.CompilerParams(
            dimension_semantics=("arbitrary",), vmem_limit_bytes=VMEM_LIMIT),
        name="inproj_ssd",
    )(x2, ln_w, w_ssd)


def _attn_kernel(q_ref, k_ref, vt_ref, g_ref, km_ref, amat_ref, srow_ref, o_ref,
                 off_scr, m_scr, a_scr, l_scr, acc_scr, *sp_scrs, nb, hg):
    s_scrs, p_scrs = sp_scrs[:hg], sp_scrs[hg:]
    i = pl.program_id(2)
    blk = lax.broadcasted_iota(jnp.int32, (nb, BLOCK), 0)
    elig = blk < i
    ones8 = jnp.ones((8, BLOCK), BF16)

    for hh in range(hg):
        hs = slice(hh * HEAD_DIM, (hh + 1) * HEAD_DIM)
        q = q_ref[:, hs]
        km_hi, km_lo = _split2(km_ref[:, hs])
        gate = _dot_nt(km_hi, q) + _dot_nt(km_lo, q)
        gate = jnp.where(elig, gate, -jnp.inf)
        rank = jnp.zeros((nb, BLOCK), F32)
        for m in range(nb):
            gm = gate[m:m + 1, :]
            beats = (gm > gate) | ((gm == gate) & (blk > m))
            rank = rank + beats.astype(F32)
        sel = elig & (rank < TOPK)
        off_scr[hh] = srow_ref[hh] * ((blk - i) * BLOCK).astype(F32) + jnp.where(sel, 0.0, NEG)

        s = _dot_nt(k_ref[i, :, hs], q) + amat_ref[hh, 1]
        m0 = jnp.max(s, axis=0, keepdims=True)
        m_scr[hh] = m0
        p_scrs[hh][1] = jnp.exp2(s - m0).astype(BF16)
        a_scr[hh] = jnp.ones((1, BLOCK), F32)
        l_scr[hh] = jnp.zeros((8, BLOCK), F32)
        acc_scr[hh] = jnp.zeros((HEAD_DIM, BLOCK), F32)
        s_scrs[hh][0] = _dot_nt(k_ref[0, :, hs], q)

    def apply_prev(hh, slot_prev, v_blk):
        hs = slice(hh * HEAD_DIM, (hh + 1) * HEAD_DIM)
        p_prev = p_scrs[hh][slot_prev]
        a_prev = a_scr[hh]
        acc_scr[hh] = a_prev * acc_scr[hh] + _dot(vt_ref[v_blk, hs, :], p_prev)
        l_scr[hh] = a_prev * l_scr[hh] + _dot(ones8, p_prev)

    def body(n, _):
        slot = lax.rem(n, 2)
        v_prev = jnp.where(n == 0, i, n - 1)
        for hh in range(hg):
            hs = slice(hh * HEAD_DIM, (hh + 1) * HEAD_DIM)
            s_scr = s_scrs[hh]
            apply_prev(hh, 1 - slot, v_prev)
            t = s_scr[slot] + amat_ref[hh, 0]
            s_scr[1 - slot] = _dot_nt(k_ref[n + 1, :, hs], q_ref[:, hs])
            off = off_scr[hh, pl.ds(n, 1), :]
            m_old = m_scr[hh]
            m_new = jnp.maximum(m_old, jnp.max(t, axis=0, keepdims=True) + off)
            p_scrs[hh][slot] = jnp.exp2(t - (m_new - off)).astype(BF16)
            a_scr[hh] = jnp.exp2(m_old - m_new)
            m_scr[hh] = m_new
        return 0

    lax.fori_loop(0, i, body, 0)
    slot_last = lax.rem(i + 1, 2)
    v_last = jnp.maximum(i - 1, 0)
    for hh in range(hg):
        hs = slice(hh * HEAD_DIM, (hh + 1) * HEAD_DIM)
        apply_prev(hh, slot_last, v_last)
        out = (acc_scr[hh] / l_scr[hh, 0:1, :]).T
        o_ref[:, hs] = (out * g_ref[:, hs].astype(F32)).astype(BF16)


def _moba_attn(qkg, vt, kmean, amat, srow, *, batch, seq, hg):
    nb = seq // BLOCK
    m = batch * seq
    hw = hg * HEAD_DIM
    ng = N_HEADS // hg
    qkg3 = qkg.reshape(m // BLOCK, BLOCK, 3 * D_ATTN)
    km2 = kmean.reshape(m // BLOCK, D_ATTN)
    return pl.pallas_call(
        functools.partial(_attn_kernel, nb=nb, hg=hg),
        grid=(batch, ng, nb),
        in_specs=[
            pl.BlockSpec((None, BLOCK, hw), lambda b, h, i: (b * nb + i, 0, h)),
            pl.BlockSpec((nb, BLOCK, hw), lambda b, h, i: (b, 0, ng + h)),
            pl.BlockSpec((nb, hw, BLOCK), lambda b, h, i: (b, h, 0)),
            pl.BlockSpec((None, BLOCK, hw), lambda b, h, i: (b * nb + i, 0, 2 * ng + h)),
            pl.BlockSpec((nb, hw), lambda b, h, i: (b, h)),
            pl.BlockSpec((hg, 2, BLOCK, BLOCK), lambda b, h, i: (h, 0, 0, 0)),
            pl.BlockSpec((hg, 1, BLOCK), lambda b, h, i: (h, 0, 0)),
        ],
        out_specs=pl.BlockSpec((None, BLOCK, hw), lambda b, h, i: (b * nb + i, 0, h)),
        out_shape=jax.ShapeDtypeStruct((m // BLOCK, BLOCK, D_ATTN), BF16),
        scratch_shapes=[
            pltpu.VMEM((hg, nb, BLOCK), F32),
            pltpu.VMEM((hg, 1, BLOCK), F32),
            pltpu.VMEM((hg, 1, BLOCK), F32),
            pltpu.VMEM((hg, 8, BLOCK), F32),
            pltpu.VMEM((hg, HEAD_DIM, BLOCK), F32),
        ] + [pltpu.VMEM((2, BLOCK, BLOCK), F32) for _ in range(hg)]
          + [pltpu.VMEM((2, BLOCK, BLOCK), BF16) for _ in range(hg)],
        compiler_params=pltpu.CompilerParams(
            dimension_semantics=("arbitrary", "arbitrary", "arbitrary"), vmem_limit_bytes=VMEM_LIMIT),
        name="moba_attn",
    )(qkg3, qkg3, vt, qkg3, km2, amat, srow).reshape(m, D_ATTN)


def _alibi_tables():
    slopes = (LOG2E * np.exp2(-8.0 * np.arange(1, N_HEADS + 1, dtype=np.float64) / N_HEADS)).astype(np.float32)
    c = np.arange(BLOCK, dtype=np.float32)
    past = slopes[:, None, None] * np.broadcast_to(c[:, None], (BLOCK, BLOCK))[None]
    causal = np.where(c[:, None] <= c[None, :], 0.0, NEG).astype(np.float32)
    amat = np.stack([past, past + causal[None]], axis=1).astype(np.float32)
    srow = np.broadcast_to(slopes[:, None, None], (N_HEADS, 1, BLOCK)).astype(np.float32)
    return jnp.asarray(amat), jnp.asarray(srow)


def _ssd_kernel(p_ref, cw_ref, cb_ref, dtb_ref, alog_ref, dsk_ref, nw_ref, tri_ref, ex_ref,
                o_ref, xpad_scr, st_scr, y_scr, *, d_ssd):
    c = pl.program_id(1)
    d_conv = d_ssd + 2 * SSD_GROUPS * SSD_STATE
    gw = d_ssd // SSD_GROUPS

    @pl.when(c == 0)
    def _():
        xpad_scr[0:8, :] = jnp.zeros((8, d_conv), F32)
        st_scr[...] = jnp.zeros_like(st_scr)

    z = p_ref[:, 0:d_ssd]
    u = p_ref[:, d_ssd:d_ssd + d_conv]
    dtr = p_ref[:, d_ssd + d_conv:d_ssd + d_conv + LANES]

    xpad_scr[8:8 + BLOCK, :] = u
    conv = cb_ref[...] + cw_ref[CONV_W - 1:CONV_W, :] * u
    for s in range(1, CONV_W):
        conv = conv + cw_ref[CONV_W - 1 - s:CONV_W - s, :] * xpad_scr[8 - s:8 - s + BLOCK, :]
    xpad_scr[0:8, :] = u[BLOCK - 8:BLOCK, :]
    act = conv * _sigmoid(conv)
    xs = act[:, 0:d_ssd]
    bm = act[:, d_ssd:d_ssd + SSD_GROUPS * SSD_STATE]
    cm = act[:, d_ssd + SSD_GROUPS * SSD_STATE:d_conv]

    v = dtr + dtb_ref[...]
    dt = jnp.maximum(v, 0.0) + jnp.log1p(jnp.exp(-jnp.abs(v)))
    da = dt * (-jnp.exp(alog_ref[...]))
    hi = da.astype(BF16)
    r1 = da - hi.astype(F32)
    mid = r1.astype(BF16)
    lo = (r1 - mid.astype(F32)).astype(BF16)
    tri = tri_ref[...]
    acs = _dot(tri, hi) + _dot(tri, mid) + _dot(tri, lo)
    acs_t = acs.T
    e1 = jnp.exp(acs)
    e2 = jnp.exp(acs[BLOCK - 1:BLOCK, :] - acs)

    def expand(a):
        a_hi, a_lo = _split2(a)
        return _dot(jnp.concatenate([a_hi, a_lo], axis=1), ex_ref[...])

    dtx = expand(dt)
    e1x = expand(e1)
    w2x = expand(dt * e2)
    xdt_b = (xs * dtx).astype(BF16)

    row = lax.broadcasted_iota(jnp.int32, (BLOCK, BLOCK), 0)
    col = lax.broadcasted_iota(jnp.int32, (BLOCK, BLOCK), 1)
    causal = row >= col
    lane = lax.broadcasted_iota(jnp.int32, (BLOCK, LANES), 1)
    heads_per_group = gw // SSD_HEAD
    for g in range(SSD_GROUPS):
        bg = bm[:, g * SSD_STATE:(g + 1) * SSD_STATE]
        cg_b = cm[:, g * SSD_STATE:(g + 1) * SSD_STATE].astype(BF16)
        gmat = _dot_nt(cg_b, bg.astype(BF16))
        for pr in range(heads_per_group // 2):
            slab = slice(g * gw + pr * LANES, g * gw + (pr + 1) * LANES)
            xp = xdt_b[:, slab]
            pair = None
            for hh in range(2):
                h = g * heads_per_group + pr * 2 + hh
                diff = acs[:, h:h + 1] - acs_t[h:h + 1, :]
                w = (gmat * jnp.exp(jnp.where(causal, diff, -jnp.inf))).astype(BF16)
                keep = (lane < SSD_HEAD) if hh == 0 else (lane >= SSD_HEAD)
                part = _dot(w, jnp.where(keep, xp, jnp.zeros_like(xp)))
                pair = part if pair is None else pair + part
            y_scr[:, slab] = pair
        gsl = slice(g * gw, (g + 1) * gw)
        st = st_scr[g]
        y_scr[:, gsl] = y_scr[:, gsl] + _dot(cg_b, st.astype(BF16)) * e1x[:, gsl]
        xw = (xs[:, gsl] * w2x[:, gsl]).astype(BF16)
        st_scr[g] = st * e1x[BLOCK - 1:BLOCK, gsl] + _dot(bg.T.astype(BF16), xw)

    y = y_scr[...] + xs * dsk_ref[...]
    y = y * (z * _sigmoid(z))
    for g in range(SSD_GROUPS):
        gsl = slice(g * gw, (g + 1) * gw)
        yg = y[:, gsl]
        ms = jnp.mean(yg * yg, axis=-1, keepdims=True)
        o_ref[:, gsl] = (yg * lax.rsqrt(ms + EPS) * nw_ref[:, gsl]).astype(BF16)


def _ssd(proj, cw, cb, dtb, alog, dsk, nw, tri, ex, *, batch, seq, d_ssd):
    nc = seq // BLOCK
    m, n = proj.shape
    d_conv = d_ssd + 2 * SSD_GROUPS * SSD_STATE
    gw = d_ssd // SSD_GROUPS
    const = lambda b, c: (0, 0)
    return pl.pallas_call(
        functools.partial(_ssd_kernel, d_ssd=d_ssd),
        grid=(batch, nc),
        in_specs=[
            pl.BlockSpec((BLOCK, n), lambda b, c: (b * nc + c, 0)),
            pl.BlockSpec((CONV_W, d_conv), const),
            pl.BlockSpec((1, d_conv), const),
            pl.BlockSpec((1, LANES), const),
            pl.BlockSpec((1, LANES), const),
            pl.BlockSpec((1, d_ssd), const),
            pl.BlockSpec((1, d_ssd), const),
            pl.BlockSpec((BLOCK, BLOCK), const),
            pl.BlockSpec((2 * LANES, d_ssd), const),
        ],
        out_specs=pl.BlockSpec((BLOCK, d_ssd), lambda b, c: (b * nc + c, 0)),
        out_shape=jax.ShapeDtypeStruct((m, d_ssd), BF16),
        scratch_shapes=[
            pltpu.VMEM((8 + BLOCK, d_conv), F32),
            pltpu.VMEM((SSD_GROUPS, SSD_STATE, gw), F32),
            pltpu.VMEM((BLOCK, d_ssd), F32),
        ],
        compiler_params=pltpu.CompilerParams(
            dimension_semantics=("arbitrary", "arbitrary"), vmem_limit_bytes=VMEM_LIMIT),
        name="ssd",
    )(proj, cw, cb, dtb, alog, dsk, nw, tri, ex)


def _outproj_kernel(a_ref, y_ref, x_ref, w_ref, o_ref):
    da = a_ref.shape[1]
    o_ref[...] = x_ref[...] + _dot(a_ref[...], w_ref[0:da, :]) + _dot(y_ref[...], w_ref[da:, :])


def _outproj(attn, y, x2, w_out, *, tm):
    m, d = x2.shape
    return pl.pallas_call(
        _outproj_kernel,
        grid=(m // tm,),
        in_specs=[
            pl.BlockSpec((tm, attn.shape[1]), lambda i: (i, 0)),
            pl.BlockSpec((tm, y.shape[1]), lambda i: (i, 0)),
            pl.BlockSpec((tm, d), lambda i: (i, 0)),
            pl.BlockSpec(w_out.shape, lambda i: (0, 0)),
        ],
        out_specs=pl.BlockSpec((tm, d), lambda i: (i, 0)),
        out_shape=jax.ShapeDtypeStruct((m, d), F32),
        compiler_params=pltpu.CompilerParams(
            dimension_semantics=("arbitrary",), vmem_limit_bytes=VMEM_LIMIT),
        name="outproj",
    )(attn, y, x2, w_out)


def _row_tile(m, want):
    t = want
    while m % t:
        t //= 2
    return t


def _layer(x2, ln_w, w_in, q_norm_w, k_norm_w, conv_w, conv_b, dt_bias, a_log, d_skip, ssd_norm_w, w_out,
           tables, *, batch, seq):
    d = x2.shape[1]
    d_ssd = w_out.shape[0] - D_ATTN
    n_ssd_heads = d_ssd // SSD_HEAD
    d_conv = d_ssd + 2 * SSD_GROUPS * SSD_STATE
    amat, srow, tri, ex = tables
    m = x2.shape[0]

    wq, wk, wv, wg = (w_in[:, k * D_ATTN:(k + 1) * D_ATTN] for k in range(4))
    w_qkg = jnp.concatenate([wq, wk, wg], axis=1).astype(BF16)
    w_vt = wv.T.astype(BF16)
    w_rest = w_in[:, 4 * D_ATTN:]
    pad = (-w_rest.shape[1]) % (3 * LANES)
    w_ssd = jnp.pad(w_rest, ((0, 0), (0, pad))).astype(BF16)
    lnw = ln_w.reshape(1, d)

    def lane_pad(v):
        return jnp.pad(v, (0, LANES - v.shape[0])).reshape(1, LANES)

    qkg, vt, kmean = _inproj_attn(x2, lnw, w_qkg, w_vt, q_norm_w.reshape(1, HEAD_DIM),
                                  k_norm_w.reshape(1, HEAD_DIM), tm=_row_tile(m, 1024))
    attn = _moba_attn(qkg, vt, kmean, amat, srow, batch=batch, seq=seq, hg=ATTN_HEADS_PER_STEP)
    proj = _inproj_ssd(x2, lnw, w_ssd, tm=_row_tile(m, 512))
    y = _ssd(proj, conv_w, conv_b.reshape(1, d_conv), lane_pad(dt_bias), lane_pad(a_log),
             jnp.repeat(d_skip, SSD_HEAD).reshape(1, d_ssd), ssd_norm_w.reshape(1, d_ssd), tri, ex,
             batch=batch, seq=seq, d_ssd=d_ssd)
    del n_ssd_heads
    return _outproj(attn, y, x2, w_out.astype(BF16), tm=_row_tile(m, 512))


def kernel(x, ln_w, w_in, q_norm_w, k_norm_w, conv_w, conv_b, dt_bias, a_log, d_skip, ssd_norm_w, w_out):
    batch, seq, d = x.shape
    assert seq % BLOCK == 0 and d == D_ATTN
    d_ssd = w_out.shape[1] - D_ATTN
    n_heads = d_ssd // SSD_HEAD
    amat, srow = _alibi_tables()
    tri = jnp.asarray(np.tril(np.ones((BLOCK, BLOCK), np.float32))).astype(BF16)
    ex_np = np.zeros((LANES, d_ssd), np.float32)
    ex_np[np.repeat(np.arange(n_heads), SSD_HEAD), np.arange(d_ssd)] = 1.0
    ex = jnp.asarray(np.concatenate([ex_np, ex_np], axis=0)).astype(BF16)
    tables = (amat, srow, tri, ex)
    h = x.reshape(batch * seq, d)
    for i in range(ln_w.shape[0]):
        h = _layer(h, ln_w[i], w_in[i], q_norm_w[i], k_norm_w[i], conv_w[i], conv_b[i], dt_bias[i],
                   a_log[i], d_skip[i], ssd_norm_w[i], w_out[i], tables, batch=batch, seq=seq)
    return h.reshape(batch, seq, d)
```

```python
import functools

import jax
import jax.numpy as jnp
import numpy as np
from jax import lax
from jax.experimental import pallas as pl
from jax.experimental.pallas import tpu as pltpu

F32 = jnp.float32
BF16 = jnp.bfloat16

HEAD_DIM = 128
N_HEADS = 8
D_ATTN = N_HEADS * HEAD_DIM
BLOCK = 256
TOPK = 3
SSD_HEAD = 64
SSD_STATE = 128
SSD_GROUPS = 2
CONV_W = 4
EPS = 1e-6
NEG = -1e30
LOG2E = 1.4426950408889634
ATTN_HEADS_PER_STEP = 8
LANES = 128
VMEM_LIMIT = 56 * 1024 * 1024


def _sigmoid(x):
    return 1.0 / (1.0 + jnp.exp(-x))


def _dot_nt(a, b):
    return lax.dot_general(a, b, (((1,), (1,)), ((), ())), preferred_element_type=F32)


def _dot(a, b):
    return jnp.dot(a, b, preferred_element_type=F32)


def _split2(x):
    hi = x.astype(BF16)
    lo = (x - hi.astype(F32)).astype(BF16)
    return hi, lo


def _inproj_attn_kernel(x_ref, lnw_ref, w_ref, wt_ref, kw_ref, qwb_ref, o_ref, qt_ref, vt_ref, km_ref, h_scr, *, tm):
    j = pl.program_id(1)
    d = D_ATTN
    nblk = tm // BLOCK

    @pl.when(j == 0)
    def _():
        x = x_ref[...]
        ms = jnp.mean(x * x, axis=-1, keepdims=True)
        h_scr[...] = (x * lax.rsqrt(ms + EPS) * lnw_ref[...]).astype(BF16)
        for c in range(d // 256):
            acc = _dot(h_scr[...], w_ref[:, c * 256:(c + 1) * 256])
            for hh in range(2):
                lo = c * 256 + hh * HEAD_DIM
                a = acc[:, hh * HEAD_DIM:(hh + 1) * HEAD_DIM]
                y = a * lax.rsqrt(jnp.mean(a * a, axis=-1, keepdims=True) + EPS) * kw_ref[...]
                o_ref[:, lo:lo + HEAD_DIM] = y.astype(BF16)
                for b in range(nblk):
                    km_ref[0, b:b + 1, lo:lo + HEAD_DIM] = jnp.mean(
                        y[b * BLOCK:(b + 1) * BLOCK], axis=0, keepdims=True)

    @pl.when(j == 1)
    def _():
        for c in range(d // 256):
            acc = _dot(h_scr[...], w_ref[:, c * 256:(c + 1) * 256])
            o_ref[:, c * 256:(c + 1) * 256] = (acc * _sigmoid(acc)).astype(BF16)

    @pl.when(j == 2)
    def _():
        for b in range(nblk):
            a = _dot_nt(wt_ref[0], h_scr[b * BLOCK:(b + 1) * BLOCK, :])
            for hh in range(N_HEADS):
                hs = slice(hh * HEAD_DIM, (hh + 1) * HEAD_DIM)
                ah = a[hs, :]
                y = ah * lax.rsqrt(jnp.mean(ah * ah, axis=0, keepdims=True) + EPS) * qwb_ref[...]
                qt_ref[b, hs, :] = y.astype(BF16)

    @pl.when(j == 3)
    def _():
        for b in range(nblk):
            vt_ref[b] = _dot_nt(wt_ref[0], h_scr[b * BLOCK:(b + 1) * BLOCK, :]).astype(BF16)


def _inproj_attn(x2, ln_w, w_kg, w_t, kw, qwb, *, tm):
    m, d = x2.shape
    nblk = tm // BLOCK
    grid = (m // tm, 4)
    t_spec = pl.BlockSpec((nblk, D_ATTN, BLOCK), lambda i, j: (i, 0, 0))
    t_shape = jax.ShapeDtypeStruct((m // BLOCK, D_ATTN, BLOCK), BF16)
    return pl.pallas_call(
        functools.partial(_inproj_attn_kernel, tm=tm),
        grid=grid,
        in_specs=[
            pl.BlockSpec((tm, d), lambda i, j: (i, 0)),
            pl.BlockSpec((1, d), lambda i, j: (0, 0)),
            pl.BlockSpec((d, D_ATTN), lambda i, j: (0, jnp.minimum(j, 1))),
            pl.BlockSpec((1, D_ATTN, d), lambda i, j: (jnp.maximum(j - 2, 0), 0, 0)),
            pl.BlockSpec((1, HEAD_DIM), lambda i, j: (0, 0)),
            pl.BlockSpec((HEAD_DIM, BLOCK), lambda i, j: (0, 0)),
        ],
        out_specs=[
            pl.BlockSpec((tm, D_ATTN), lambda i, j: (i, jnp.minimum(j, 1))),
            t_spec,
            t_spec,
            pl.BlockSpec((1, nblk, D_ATTN), lambda i, j: (i, 0, 0)),
        ],
        out_shape=[
            jax.ShapeDtypeStruct((m, 2 * D_ATTN), BF16),
            t_shape,
            t_shape,
            jax.ShapeDtypeStruct((m // tm, nblk, D_ATTN), F32),
        ],
        scratch_shapes=[pltpu.VMEM((tm, d), BF16)],
        compiler_params=pltpu.CompilerParams(
            dimension_semantics=("arbitrary", "arbitrary"), vmem_limit_bytes=VMEM_LIMIT),
        name="inproj_attn",
    )(x2, ln_w, w_kg, w_t, kw, qwb)


def _inproj_ssd_kernel(x_ref, lnw_ref, w_ref, o_ref):
    x = x_ref[...]
    ms = jnp.mean(x * x, axis=-1, keepdims=True)
    h = (x * lax.rsqrt(ms + EPS) * lnw_ref[...]).astype(BF16)
    n = w_ref.shape[1]
    for c in range(n // LANES // 3):
        sl = slice(c * 3 * LANES, (c + 1) * 3 * LANES)
        o_ref[:, sl] = _dot(h, w_ref[:, sl])


def _inproj_ssd(x2, ln_w, w_ssd, *, tm):
    m, d = x2.shape
    n = w_ssd.shape[1]
    return pl.pallas_call(
        _inproj_ssd_kernel,
        grid=(m // tm,),
        in_specs=[
            pl.BlockSpec((tm, d), lambda i: (i, 0)),
            pl.BlockSpec((1, d), lambda i: (0, 0)),
            pl.BlockSpec((d, n), lambda i: (0, 0)),
        ],
        out_specs=pl.BlockSpec((tm, n), lambda i: (i, 0)),
        out_shape=jax.ShapeDtypeStruct((m, n), F32),
        compiler_params=pltpu.CompilerParams(
            dimension_semantics=("arbitrary",), vmem_limit_bytes=VMEM_LIMIT),
        name="inproj_ssd",
    )(x2, ln_w, w_ssd)


def _attn_kernel(qt_ref, k_ref, vt_ref, g_ref, km_ref, amat_ref, srow_ref, o_ref,
                 off_scr, mx_scr, m_scr, a_scr, l_scr, acc_scr, *sp_scrs, nb, hg):
    s_scrs, p_scrs = sp_scrs[:hg], sp_scrs[hg:]
    i = pl.program_id(2)
    blk = lax.broadcasted_iota(jnp.int32, (nb, BLOCK), 0)
    elig = blk < i
    ones8 = jnp.ones((8, BLOCK), BF16)

    def scores(hh, k_blk, causal):
        hs = slice(hh * HEAD_DIM, (hh + 1) * HEAD_DIM)
        s = _dot(k_ref[k_blk, :, hs], qt_ref[hs, :]) + amat_ref[hh, causal]
        s_scrs[hh][...] = s
        mx_scr[hh] = jnp.max(s, axis=0, keepdims=True)

    for hh in range(hg):
        hs = slice(hh * HEAD_DIM, (hh + 1) * HEAD_DIM)
        qt = qt_ref[hs, :]
        km_hi, km_lo = _split2(km_ref[:, hs])
        gate = _dot(km_hi, qt) + _dot(km_lo, qt)
        gate = jnp.where(elig, gate, -jnp.inf)
        rank = jnp.zeros((nb, BLOCK), F32)
        for m in range(nb):
            gm = gate[m:m + 1, :]
            beats = (gm > gate) | ((gm == gate) & (blk > m))
            rank = rank + beats.astype(F32)
        sel = elig & (rank < TOPK)
        off = srow_ref[hh] * ((blk - i) * BLOCK).astype(F32) + jnp.where(sel, 0.0, NEG)
        off_scr[hh] = jnp.where(blk == 0, 0.0, pltpu.roll(off, 1, axis=0))
        scores(hh, i, 1)
        m_scr[hh] = jnp.full((1, BLOCK), NEG, F32)
        a_scr[hh] = jnp.ones((1, BLOCK), F32)
        l_scr[hh] = jnp.zeros((8, BLOCK), F32)
        acc_scr[hh] = jnp.zeros((HEAD_DIM, BLOCK), F32)
        p_scrs[hh][...] = jnp.zeros((BLOCK, BLOCK), BF16)

    def apply_prev(hh, v_blk):
        hs = slice(hh * HEAD_DIM, (hh + 1) * HEAD_DIM)
        p_prev = p_scrs[hh][...]
        a_prev = a_scr[hh]
        acc_scr[hh] = a_prev * acc_scr[hh] + _dot(vt_ref[v_blk, hs, :], p_prev)
        l_scr[hh] = a_prev * l_scr[hh] + _dot(ones8, p_prev)

    def body(j, _):
        v_prev = jnp.where(j == 1, i, jnp.maximum(j - 2, 0))
        for hh in range(hg):
            apply_prev(hh, v_prev)
            off = off_scr[hh, pl.ds(j, 1), :]
            m_old = m_scr[hh]
            m_new = jnp.maximum(m_old, mx_scr[hh] + off)
            p_scrs[hh][...] = jnp.exp2(s_scrs[hh][...] - (m_new - off)).astype(BF16)
            a_scr[hh] = jnp.exp2(m_old - m_new)
            m_scr[hh] = m_new
            scores(hh, j, 0)
        return 0

    lax.fori_loop(0, i + 1, body, 0)
    v_last = jnp.maximum(i - 1, 0)
    for hh in range(hg):
        hs = slice(hh * HEAD_DIM, (hh + 1) * HEAD_DIM)
        apply_prev(hh, v_last)
        out = (acc_scr[hh] / l_scr[hh, 0:1, :]).T
        o_ref[:, hs] = (out * g_ref[:, hs].astype(F32)).astype(BF16)


def _moba_attn(kg, qt, vt, kmean, amat, srow, *, batch, seq, hg):
    nb = seq // BLOCK
    m = batch * seq
    hw = hg * HEAD_DIM
    ng = N_HEADS // hg
    kg3 = kg.reshape(m // BLOCK, BLOCK, 2 * D_ATTN)
    km2 = kmean.reshape(m // BLOCK, D_ATTN)
    return pl.pallas_call(
        functools.partial(_attn_kernel, nb=nb, hg=hg),
        grid=(batch, ng, nb),
        in_specs=[
            pl.BlockSpec((None, hw, BLOCK), lambda b, h, i: (b * nb + i, h, 0)),
            pl.BlockSpec((nb, BLOCK, hw), lambda b, h, i: (b, 0, h)),
            pl.BlockSpec((nb, hw, BLOCK), lambda b, h, i: (b, h, 0)),
            pl.BlockSpec((None, BLOCK, hw), lambda b, h, i: (b * nb + i, 0, ng + h)),
            pl.BlockSpec((nb, hw), lambda b, h, i: (b, h)),
            pl.BlockSpec((hg, 2, BLOCK, BLOCK), lambda b, h, i: (h, 0, 0, 0)),
            pl.BlockSpec((hg, 1, BLOCK), lambda b, h, i: (h, 0, 0)),
        ],
        out_specs=pl.BlockSpec((None, BLOCK, hw), lambda b, h, i: (b * nb + i, 0, h)),
        out_shape=jax.ShapeDtypeStruct((m // BLOCK, BLOCK, D_ATTN), BF16),
        scratch_shapes=[
            pltpu.VMEM((hg, nb, BLOCK), F32),
            pltpu.VMEM((hg, 1, BLOCK), F32),
            pltpu.VMEM((hg, 1, BLOCK), F32),
            pltpu.VMEM((hg, 1, BLOCK), F32),
            pltpu.VMEM((hg, 8, BLOCK), F32),
            pltpu.VMEM((hg, HEAD_DIM, BLOCK), F32),
        ] + [pltpu.VMEM((BLOCK, BLOCK), F32) for _ in range(hg)]
          + [pltpu.VMEM((BLOCK, BLOCK), BF16) for _ in range(hg)],
        compiler_params=pltpu.CompilerParams(
            dimension_semantics=("arbitrary", "arbitrary", "arbitrary"), vmem_limit_bytes=VMEM_LIMIT),
        name="moba_attn",
    )(qt, kg3, vt, kg3, km2, amat, srow).reshape(m, D_ATTN)


def _alibi_tables():
    slopes = (LOG2E * np.exp2(-8.0 * np.arange(1, N_HEADS + 1, dtype=np.float64) / N_HEADS)).astype(np.float32)
    c = np.arange(BLOCK, dtype=np.float32)
    past = slopes[:, None, None] * np.broadcast_to(c[:, None], (BLOCK, BLOCK))[None]
    causal = np.where(c[:, None] <= c[None, :], 0.0, NEG).astype(np.float32)
    amat = np.stack([past, past + causal[None]], axis=1).astype(np.float32)
    srow = np.broadcast_to(slopes[:, None, None], (N_HEADS, 1, BLOCK)).astype(np.float32)
    return jnp.asarray(amat), jnp.asarray(srow)


def _ssd_kernel(p_ref, cw_ref, cb_ref, dtb_ref, alog_ref, dsk_ref, nw_ref, tri_ref, ex_ref,
                o_ref, xpad_scr, st_scr, y_scr, *, d_ssd):
    c = pl.program_id(1)
    d_conv = d_ssd + 2 * SSD_GROUPS * SSD_STATE
    gw = d_ssd // SSD_GROUPS

    @pl.when(c == 0)
    def _():
        xpad_scr[0:8, :] = jnp.zeros((8, d_conv), F32)
        st_scr[...] = jnp.zeros_like(st_scr)

    z = p_ref[:, 0:d_ssd]
    u = p_ref[:, d_ssd:d_ssd + d_conv]
    dtr = p_ref[:, d_ssd + d_conv:d_ssd + d_conv + LANES]

    xpad_scr[8:8 + BLOCK, :] = u
    conv = cb_ref[...] + cw_ref[CONV_W - 1:CONV_W, :] * u
    for s in range(1, CONV_W):
        conv = conv + cw_ref[CONV_W - 1 - s:CONV_W - s, :] * xpad_scr[8 - s:8 - s + BLOCK, :]
    xpad_scr[0:8, :] = u[BLOCK - 8:BLOCK, :]
    act = conv * _sigmoid(conv)
    xs = act[:, 0:d_ssd]
    bm = act[:, d_ssd:d_ssd + SSD_GROUPS * SSD_STATE]
    cm = act[:, d_ssd + SSD_GROUPS * SSD_STATE:d_conv]

    v = dtr + dtb_ref[...]
    dt = jnp.maximum(v, 0.0) + jnp.log1p(jnp.exp(-jnp.abs(v)))
    da = dt * (-jnp.exp(alog_ref[...]))
    hi = da.astype(BF16)
    r1 = da - hi.astype(F32)
    mid = r1.astype(BF16)
    lo = (r1 - mid.astype(F32)).astype(BF16)
    tri = tri_ref[...]
    acs = _dot(tri, hi) + _dot(tri, mid) + _dot(tri, lo)
    acs_t = acs.T
    e1 = jnp.exp(acs)
    e2 = jnp.exp(acs[BLOCK - 1:BLOCK, :] - acs)

    def expand(a):
        a_hi, a_lo = _split2(a)
        return _dot(jnp.concatenate([a_hi, a_lo], axis=1), ex_ref[...])

    dtx = expand(dt)
    e1x = expand(e1)
    w2x = expand(dt * e2)
    xdt_b = (xs * dtx).astype(BF16)

    row = lax.broadcasted_iota(jnp.int32, (BLOCK, BLOCK), 0)
    col = lax.broadcasted_iota(jnp.int32, (BLOCK, BLOCK), 1)
    causal = row >= col
    lane = lax.broadcasted_iota(jnp.int32, (BLOCK, LANES), 1)
    heads_per_group = gw // SSD_HEAD
    for g in range(SSD_GROUPS):
        bg = bm[:, g * SSD_STATE:(g + 1) * SSD_STATE]
        cg_b = cm[:, g * SSD_STATE:(g + 1) * SSD_STATE].astype(BF16)
        gmat = _dot_nt(cg_b, bg.astype(BF16))
        for pr in range(heads_per_group // 2):
            slab = slice(g * gw + pr * LANES, g * gw + (pr + 1) * LANES)
            xp = xdt_b[:, slab]
            pair = None
            for hh in range(2):
                h = g * heads_per_group + pr * 2 + hh
                diff = acs[:, h:h + 1] - acs_t[h:h + 1, :]
                w = (gmat * jnp.exp(jnp.where(causal, diff, -jnp.inf))).astype(BF16)
                keep = (lane < SSD_HEAD) if hh == 0 else (lane >= SSD_HEAD)
                part = _dot(w, jnp.where(keep, xp, jnp.zeros_like(xp)))
                pair = part if pair is None else pair + part
            y_scr[:, slab] = pair
        gsl = slice(g * gw, (g + 1) * gw)
        st = st_scr[g]
        y_scr[:, gsl] = y_scr[:, gsl] + _dot(cg_b, st.astype(BF16)) * e1x[:, gsl]
        xw = (xs[:, gsl] * w2x[:, gsl]).astype(BF16)
        st_scr[g] = st * e1x[BLOCK - 1:BLOCK, gsl] + _dot(bg.T.astype(BF16), xw)

    y = y_scr[...] + xs * dsk_ref[...]
    y = y * (z * _sigmoid(z))
    for g in range(SSD_GROUPS):
        gsl = slice(g * gw, (g + 1) * gw)
        yg = y[:, gsl]
        ms = jnp.mean(yg * yg, axis=-1, keepdims=True)
        o_ref[:, gsl] = (yg * lax.rsqrt(ms + EPS) * nw_ref[:, gsl]).astype(BF16)


def _ssd(proj, cw, cb, dtb, alog, dsk, nw, tri, ex, *, batch, seq, d_ssd):
    nc = seq // BLOCK
    m, n = proj.shape
    d_conv = d_ssd + 2 * SSD_GROUPS * SSD_STATE
    gw = d_ssd // SSD_GROUPS
    const = lambda b, c: (0, 0)
    return pl.pallas_call(
        functools.partial(_ssd_kernel, d_ssd=d_ssd),
        grid=(batch, nc),
        in_specs=[
            pl.BlockSpec((BLOCK, n), lambda b, c: (b * nc + c, 0)),
            pl.BlockSpec((CONV_W, d_conv), const),
            pl.BlockSpec((1, d_conv), const),
            pl.BlockSpec((1, LANES), const),
            pl.BlockSpec((1, LANES), const),
            pl.BlockSpec((1, d_ssd), const),
            pl.BlockSpec((1, d_ssd), const),
            pl.BlockSpec((BLOCK, BLOCK), const),
            pl.BlockSpec((2 * LANES, d_ssd), const),
        ],
        out_specs=pl.BlockSpec((BLOCK, d_ssd), lambda b, c: (b * nc + c, 0)),
        out_shape=jax.ShapeDtypeStruct((m, d_ssd), BF16),
        scratch_shapes=[
            pltpu.VMEM((8 + BLOCK, d_conv), F32),
            pltpu.VMEM((SSD_GROUPS, SSD_STATE, gw), F32),
            pltpu.VMEM((BLOCK, d_ssd), F32),
        ],
        compiler_params=pltpu.CompilerParams(
            dimension_semantics=("arbitrary", "arbitrary"), vmem_limit_bytes=VMEM_LIMIT),
        name="ssd",
    )(proj, cw, cb, dtb, alog, dsk, nw, tri, ex)


def _outproj_kernel(a_ref, y_ref, x_ref, w_ref, o_ref):
    da = a_ref.shape[1]
    o_ref[...] = x_ref[...] + _dot(a_ref[...], w_ref[0:da, :]) + _dot(y_ref[...], w_ref[da:, :])


def _outproj(attn, y, x2, w_out, *, tm):
    m, d = x2.shape
    return pl.pallas_call(
        _outproj_kernel,
        grid=(m // tm,),
        in_specs=[
            pl.BlockSpec((tm, attn.shape[1]), lambda i: (i, 0)),
            pl.BlockSpec((tm, y.shape[1]), lambda i: (i, 0)),
            pl.BlockSpec((tm, d), lambda i: (i, 0)),
            pl.BlockSpec(w_out.shape, lambda i: (0, 0)),
        ],
        out_specs=pl.BlockSpec((tm, d), lambda i: (i, 0)),
        out_shape=jax.ShapeDtypeStruct((m, d), F32),
        compiler_params=pltpu.CompilerParams(
            dimension_semantics=("arbitrary",), vmem_limit_bytes=VMEM_LIMIT),
        name="outproj",
    )(attn, y, x2, w_out)


def _row_tile(m, want):
    t = want
    while m % t:
        t //= 2
    return t


def _layer(x2, ln_w, w_in, q_norm_w, k_norm_w, conv_w, conv_b, dt_bias, a_log, d_skip, ssd_norm_w, w_out,
           tables, *, batch, seq):
    d = x2.shape[1]
    d_ssd = w_out.shape[0] - D_ATTN
    n_ssd_heads = d_ssd // SSD_HEAD
    d_conv = d_ssd + 2 * SSD_GROUPS * SSD_STATE
    amat, srow, tri, ex = tables
    m = x2.shape[0]

    wq, wk, wv, wg = (w_in[:, k * D_ATTN:(k + 1) * D_ATTN] for k in range(4))
    w_kg = jnp.concatenate([wk, wg], axis=1).astype(BF16)
    w_t = jnp.stack([wq.T, wv.T]).astype(BF16)
    qwb = jnp.broadcast_to((q_norm_w * (HEAD_DIM ** -0.5 * LOG2E))[:, None], (HEAD_DIM, BLOCK))
    w_rest = w_in[:, 4 * D_ATTN:]
    pad = (-w_rest.shape[1]) % (3 * LANES)
    w_ssd = jnp.pad(w_rest, ((0, 0), (0, pad))).astype(BF16)
    lnw = ln_w.reshape(1, d)

    def lane_pad(v):
        return jnp.pad(v, (0, LANES - v.shape[0])).reshape(1, LANES)

    kg, qt, vt, kmean = _inproj_attn(x2, lnw, w_kg, w_t, k_norm_w.reshape(1, HEAD_DIM), qwb,
                                     tm=_row_tile(m, 1024))
    attn = _moba_attn(kg, qt, vt, kmean, amat, srow, batch=batch, seq=seq, hg=ATTN_HEADS_PER_STEP)
    proj = _inproj_ssd(x2, lnw, w_ssd, tm=_row_tile(m, 512))
    y = _ssd(proj, conv_w, conv_b.reshape(1, d_conv), lane_pad(dt_bias), lane_pad(a_log),
             jnp.repeat(d_skip, SSD_HEAD).reshape(1, d_ssd), ssd_norm_w.reshape(1, d_ssd), tri, ex,
             batch=batch, seq=seq, d_ssd=d_ssd)
    del n_ssd_heads
    return _outproj(attn, y, x2, w_out.astype(BF16), tm=_row_tile(m, 512))


def kernel(x, ln_w, w_in, q_norm_w, k_norm_w, conv_w, conv_b, dt_bias, a_log, d_skip, ssd_norm_w, w_out):
    batch, seq, d = x.shape
    assert seq % BLOCK == 0 and d == D_ATTN
    d_ssd = w_out.shape[1] - D_ATTN
    n_heads = d_ssd // SSD_HEAD
    amat, srow = _alibi_tables()
    tri = jnp.asarray(np.tril(np.ones((BLOCK, BLOCK), np.float32))).astype(BF16)
    ex_np = np.zeros((LANES, d_ssd), np.float32)
    ex_np[np.repeat(np.arange(n_heads), SSD_HEAD), np.arange(d_ssd)] = 1.0
    ex = jnp.asarray(np.concatenate([ex_np, ex_np], axis=0)).astype(BF16)
    tables = (amat, srow, tri, ex)
    h = x.reshape(batch * seq, d)
    for i in range(ln_w.shape[0]):
        h = _layer(h, ln_w[i], w_in[i], q_norm_w[i], k_norm_w[i], conv_w[i], conv_b[i], dt_bias[i],
                   a_log[i], d_skip[i], ssd_norm_w[i], w_out[i], tables, batch=batch, seq=seq)
    return h.reshape(batch, seq, d)
```

```python
import functools

import jax
import jax.numpy as jnp
import numpy as np
from jax import lax
from jax.experimental import pallas as pl
from jax.experimental.pallas import tpu as pltpu

F32 = jnp.float32
BF16 = jnp.bfloat16

HEAD_DIM = 128
N_HEADS = 8
D_ATTN = N_HEADS * HEAD_DIM
BLOCK = 256
TOPK = 3
SSD_HEAD = 64
SSD_STATE = 128
SSD_GROUPS = 2
CONV_W = 4
EPS = 1e-6
NEG = -1e30
LOG2E = 1.4426950408889634
ATTN_HEADS_PER_STEP = 8
LANES = 128
MXU_COLS = 256
VMEM_LIMIT = 56 * 1024 * 1024


def _sigmoid(x):
    return 1.0 / (1.0 + jnp.exp(-x))


def _dot_nt(a, b):
    return lax.dot_general(a, b, (((1,), (1,)), ((), ())), preferred_element_type=F32)


def _dot(a, b):
    return jnp.dot(a, b, preferred_element_type=F32)


def _split2(x):
    hi = x.astype(BF16)
    lo = (x - hi.astype(F32)).astype(BF16)
    return hi, lo


def _inproj_attn_kernel(x_ref, lnw_ref, w_ref, wt_ref, kw_ref, qwb_ref, o_ref, qt_ref, vt_ref, km_ref, h_scr, *, tm):
    j = pl.program_id(1)
    d = D_ATTN
    nblk = tm // BLOCK

    @pl.when(j == 0)
    def _():
        x = x_ref[...]
        ms = jnp.mean(x * x, axis=-1, keepdims=True)
        h_scr[...] = (x * lax.rsqrt(ms + EPS) * lnw_ref[...]).astype(BF16)
        for c in range(d // 256):
            acc = _dot(h_scr[...], w_ref[:, c * 256:(c + 1) * 256])
            for hh in range(2):
                lo = c * 256 + hh * HEAD_DIM
                a = acc[:, hh * HEAD_DIM:(hh + 1) * HEAD_DIM]
                y = a * lax.rsqrt(jnp.mean(a * a, axis=-1, keepdims=True) + EPS) * kw_ref[...]
                o_ref[:, lo:lo + HEAD_DIM] = y.astype(BF16)
                for b in range(nblk):
                    km_ref[0, b:b + 1, lo:lo + HEAD_DIM] = jnp.mean(
                        y[b * BLOCK:(b + 1) * BLOCK], axis=0, keepdims=True)

    @pl.when(j == 1)
    def _():
        for c in range(d // 256):
            acc = _dot(h_scr[...], w_ref[:, c * 256:(c + 1) * 256])
            o_ref[:, c * 256:(c + 1) * 256] = (acc * _sigmoid(acc)).astype(BF16)

    @pl.when(j == 2)
    def _():
        for b in range(nblk):
            a = _dot_nt(wt_ref[0], h_scr[b * BLOCK:(b + 1) * BLOCK, :])
            for hh in range(N_HEADS):
                hs = slice(hh * HEAD_DIM, (hh + 1) * HEAD_DIM)
                ah = a[hs, :]
                y = ah * lax.rsqrt(jnp.mean(ah * ah, axis=0, keepdims=True) + EPS) * qwb_ref[...]
                qt_ref[b, hs, :] = y.astype(BF16)

    @pl.when(j == 3)
    def _():
        for b in range(nblk):
            vt_ref[b] = _dot_nt(wt_ref[0], h_scr[b * BLOCK:(b + 1) * BLOCK, :]).astype(BF16)


def _inproj_attn(x2, ln_w, w_kg, w_t, kw, qwb, *, tm):
    m, d = x2.shape
    nblk = tm // BLOCK
    grid = (m // tm, 4)
    t_spec = pl.BlockSpec((nblk, D_ATTN, BLOCK), lambda i, j: (i, 0, 0))
    t_shape = jax.ShapeDtypeStruct((m // BLOCK, D_ATTN, BLOCK), BF16)
    return pl.pallas_call(
        functools.partial(_inproj_attn_kernel, tm=tm),
        grid=grid,
        in_specs=[
            pl.BlockSpec((tm, d), lambda i, j: (i, 0)),
            pl.BlockSpec((1, d), lambda i, j: (0, 0)),
            pl.BlockSpec((d, D_ATTN), lambda i, j: (0, jnp.minimum(j, 1))),
            pl.BlockSpec((1, D_ATTN, d), lambda i, j: (jnp.maximum(j - 2, 0), 0, 0)),
            pl.BlockSpec((1, HEAD_DIM), lambda i, j: (0, 0)),
            pl.BlockSpec((HEAD_DIM, BLOCK), lambda i, j: (0, 0)),
        ],
        out_specs=[
            pl.BlockSpec((tm, D_ATTN), lambda i, j: (i, jnp.minimum(j, 1))),
            t_spec,
            t_spec,
            pl.BlockSpec((1, nblk, D_ATTN), lambda i, j: (i, 0, 0)),
        ],
        out_shape=[
            jax.ShapeDtypeStruct((m, 2 * D_ATTN), BF16),
            t_shape,
            t_shape,
            jax.ShapeDtypeStruct((m // tm, nblk, D_ATTN), F32),
        ],
        scratch_shapes=[pltpu.VMEM((tm, d), BF16)],
        compiler_params=pltpu.CompilerParams(
            dimension_semantics=("arbitrary", "arbitrary"), vmem_limit_bytes=VMEM_LIMIT),
        name="inproj_attn",
    )(x2, ln_w, w_kg, w_t, kw, qwb)


def _inproj_ssd_kernel(x_ref, lnw_ref, w_ref, o_ref):
    x = x_ref[...]
    ms = jnp.mean(x * x, axis=-1, keepdims=True)
    h = (x * lax.rsqrt(ms + EPS) * lnw_ref[...]).astype(BF16)
    n = w_ref.shape[1]
    step = 2 * MXU_COLS
    for lo in range(0, n, step):
        sl = slice(lo, min(lo + step, n))
        o_ref[:, sl] = _dot(h, w_ref[:, sl])


def _inproj_ssd(x2, ln_w, w_ssd, *, tm):
    m, d = x2.shape
    n = w_ssd.shape[1]
    return pl.pallas_call(
        _inproj_ssd_kernel,
        grid=(m // tm,),
        in_specs=[
            pl.BlockSpec((tm, d), lambda i: (i, 0)),
            pl.BlockSpec((1, d), lambda i: (0, 0)),
            pl.BlockSpec((d, n), lambda i: (0, 0)),
        ],
        out_specs=pl.BlockSpec((tm, n), lambda i: (i, 0)),
        out_shape=jax.ShapeDtypeStruct((m, n), F32),
        compiler_params=pltpu.CompilerParams(
            dimension_semantics=("arbitrary",), vmem_limit_bytes=VMEM_LIMIT),
        name="inproj_ssd",
    )(x2, ln_w, w_ssd)


def _attn_kernel(qt_ref, k_ref, vt_ref, g_ref, km_ref, acol_ref, adiag_ref, srow_ref, o_ref,
                 off_scr, mx_scr, m_scr, a_scr, l_scr, acc_scr, *sp_scrs, nb, hg):
    s_scrs, p_scrs = sp_scrs[:hg], sp_scrs[hg:]
    i = pl.program_id(2)
    blk = lax.broadcasted_iota(jnp.int32, (nb, BLOCK), 0)
    blk_f = blk.astype(F32)
    elig = blk < i
    ones16 = jnp.ones((16, BLOCK), BF16)

    def scores(hh, k_blk, causal):
        hs = slice(hh * HEAD_DIM, (hh + 1) * HEAD_DIM)
        if causal:
            bias = adiag_ref[hh]
        else:
            bias = jnp.concatenate([acol_ref[hh]] * (BLOCK // LANES), axis=1)
        s = _dot(k_ref[k_blk, :, hs], qt_ref[hs, :]) + bias
        s_scrs[hh][...] = s
        mx_scr[hh] = jnp.max(s, axis=0, keepdims=True)

    for hh in range(hg):
        hs = slice(hh * HEAD_DIM, (hh + 1) * HEAD_DIM)
        qt = qt_ref[hs, :]
        km_hi, km_lo = _split2(km_ref[:, hs])
        gate = _dot(km_hi, qt) + _dot(km_lo, qt)
        left = elig
        for _ in range(TOPK):
            g_left = jnp.where(left, gate, -jnp.inf)
            top = (g_left == jnp.max(g_left, axis=0, keepdims=True)) & left
            first = jnp.min(jnp.where(top, blk_f, float(nb)), axis=0, keepdims=True)
            left = left & (blk_f != first)
        sel = elig & jnp.logical_not(left)
        off = srow_ref[hh] * ((blk - i) * BLOCK).astype(F32) + jnp.where(sel, 0.0, NEG)
        off_scr[hh] = jnp.where(blk == 0, 0.0, pltpu.roll(off, 1, axis=0))
        scores(hh, i, 1)
        m_scr[hh] = jnp.full((1, BLOCK), NEG, F32)
        a_scr[hh] = jnp.ones((1, BLOCK), F32)
        l_scr[hh] = jnp.zeros((8, BLOCK), F32)
        acc_scr[hh] = jnp.zeros((HEAD_DIM, BLOCK), F32)
        p_scrs[hh][...] = jnp.zeros((BLOCK, BLOCK), BF16)

    def apply_prev(hh, v_blk):
        hs = slice(hh * HEAD_DIM, (hh + 1) * HEAD_DIM)
        p_prev = p_scrs[hh][...]
        a_prev = a_scr[hh]
        pv = _dot(jnp.concatenate([vt_ref[v_blk, hs, :], ones16], axis=0), p_prev)
        acc_scr[hh] = a_prev * acc_scr[hh] + pv[0:HEAD_DIM]
        l_scr[hh] = a_prev * l_scr[hh] + pv[HEAD_DIM:HEAD_DIM + 8]

    def body(j, _):
        v_prev = jnp.where(j == 1, i, jnp.maximum(j - 2, 0))
        for hh in range(hg):
            apply_prev(hh, v_prev)
            off = off_scr[hh, pl.ds(j, 1), :]
            m_old = m_scr[hh]
            m_new = jnp.maximum(m_old, mx_scr[hh] + off)
            p_scrs[hh][...] = jnp.exp2(s_scrs[hh][...] - (m_new - off)).astype(BF16)
            a_scr[hh] = jnp.exp2(m_old - m_new)
            m_scr[hh] = m_new
            scores(hh, j, 0)
        return 0

    lax.fori_loop(0, i + 1, body, 0)
    v_last = jnp.maximum(i - 1, 0)
    for hh in range(hg):
        hs = slice(hh * HEAD_DIM, (hh + 1) * HEAD_DIM)
        apply_prev(hh, v_last)
        out = (acc_scr[hh] / l_scr[hh, 0:1, :]).T
        o_ref[:, hs] = (out * g_ref[:, hs].astype(F32)).astype(BF16)


def _moba_attn(kg, qt, vt, kmean, alibi, *, batch, seq, hg):
    nb = seq // BLOCK
    m = batch * seq
    hw = hg * HEAD_DIM
    ng = N_HEADS // hg
    kg3 = kg.reshape(m // BLOCK, BLOCK, 2 * D_ATTN)
    km2 = kmean.reshape(m // BLOCK, D_ATTN)
    return pl.pallas_call(
        functools.partial(_attn_kernel, nb=nb, hg=hg),
        grid=(batch, ng, nb),
        in_specs=[
            pl.BlockSpec((None, hw, BLOCK), lambda b, h, i: (b * nb + i, h, 0)),
            pl.BlockSpec((nb, BLOCK, hw), lambda b, h, i: (b, 0, h)),
            pl.BlockSpec((nb, hw, BLOCK), lambda b, h, i: (b, h, 0)),
            pl.BlockSpec((None, BLOCK, hw), lambda b, h, i: (b * nb + i, 0, ng + h)),
            pl.BlockSpec((nb, hw), lambda b, h, i: (b, h)),
            pl.BlockSpec((hg, BLOCK, LANES), lambda b, h, i: (h, 0, 0)),
            pl.BlockSpec((hg, BLOCK, BLOCK), lambda b, h, i: (h, 0, 0)),
            pl.BlockSpec((hg, 1, BLOCK), lambda b, h, i: (h, 0, 0)),
        ],
        out_specs=pl.BlockSpec((None, BLOCK, hw), lambda b, h, i: (b * nb + i, 0, h)),
        out_shape=jax.ShapeDtypeStruct((m // BLOCK, BLOCK, D_ATTN), BF16),
        scratch_shapes=[
            pltpu.VMEM((hg, nb, BLOCK), F32),
            pltpu.VMEM((hg, 1, BLOCK), F32),
            pltpu.VMEM((hg, 1, BLOCK), F32),
            pltpu.VMEM((hg, 1, BLOCK), F32),
            pltpu.VMEM((hg, 8, BLOCK), F32),
            pltpu.VMEM((hg, HEAD_DIM, BLOCK), F32),
        ] + [pltpu.VMEM((BLOCK, BLOCK), F32) for _ in range(hg)]
          + [pltpu.VMEM((BLOCK, BLOCK), BF16) for _ in range(hg)],
        compiler_params=pltpu.CompilerParams(
            dimension_semantics=("arbitrary", "arbitrary", "arbitrary"), vmem_limit_bytes=VMEM_LIMIT),
        name="moba_attn",
    )(qt, kg3, vt, kg3, km2, *alibi).reshape(m, D_ATTN)


def _alibi_tables():
    slopes = (LOG2E * np.exp2(-8.0 * np.arange(1, N_HEADS + 1, dtype=np.float64) / N_HEADS)).astype(np.float32)
    c = np.arange(BLOCK, dtype=np.float32)
    past = slopes[:, None, None] * np.broadcast_to(c[:, None], (BLOCK, BLOCK))[None]
    causal = np.where(c[:, None] <= c[None, :], 0.0, NEG).astype(np.float32)
    acol = past[:, :, :LANES].astype(np.float32)
    adiag = (past + causal[None]).astype(np.float32)
    srow = np.broadcast_to(slopes[:, None, None], (N_HEADS, 1, BLOCK)).astype(np.float32)
    return jnp.asarray(acol), jnp.asarray(adiag), jnp.asarray(srow)


def _ssd_kernel(p_ref, cw_ref, cb_ref, dtb_ref, alog_ref, dsk_ref, nw_ref, tri_ref, ex_ref,
                o_ref, xpad_scr, st_scr, y_scr, *, d_ssd):
    c = pl.program_id(1)
    d_conv = d_ssd + 2 * SSD_GROUPS * SSD_STATE
    gw = d_ssd // SSD_GROUPS

    @pl.when(c == 0)
    def _():
        xpad_scr[0:8, :] = jnp.zeros((8, d_conv), F32)
        st_scr[...] = jnp.zeros_like(st_scr)

    z = p_ref[:, 0:d_ssd]
    u = p_ref[:, d_ssd:d_ssd + d_conv]
    dtr = p_ref[:, d_ssd + d_conv:d_ssd + d_conv + LANES]

    xpad_scr[8:8 + BLOCK, :] = u
    conv = cb_ref[...] + cw_ref[CONV_W - 1:CONV_W, :] * u
    for s in range(1, CONV_W):
        conv = conv + cw_ref[CONV_W - 1 - s:CONV_W - s, :] * xpad_scr[8 - s:8 - s + BLOCK, :]
    xpad_scr[0:8, :] = u[BLOCK - 8:BLOCK, :]
    act = conv * _sigmoid(conv)
    xs = act[:, 0:d_ssd]
    bm = act[:, d_ssd:d_ssd + SSD_GROUPS * SSD_STATE]
    cm = act[:, d_ssd + SSD_GROUPS * SSD_STATE:d_conv]

    v = dtr + dtb_ref[...]
    dt = jnp.maximum(v, 0.0) + jnp.log1p(jnp.exp(-jnp.abs(v)))
    da = dt * (-jnp.exp(alog_ref[...]))
    hi = da.astype(BF16)
    r1 = da - hi.astype(F32)
    mid = r1.astype(BF16)
    lo = (r1 - mid.astype(F32)).astype(BF16)
    tri = tri_ref[...]
    acs = _dot(tri, hi) + _dot(tri, mid) + _dot(tri, lo)
    acs_t = acs.T
    e1 = jnp.exp(acs)
    e2 = jnp.exp(acs[BLOCK - 1:BLOCK, :] - acs)

    def expand(a):
        a_hi, a_lo = _split2(a)
        return _dot(jnp.concatenate([a_hi, a_lo], axis=1), ex_ref[...])

    dtx = expand(dt)
    e1x = expand(e1)
    w2x = expand(dt * e2)
    xdt_b = (xs * dtx).astype(BF16)

    row = lax.broadcasted_iota(jnp.int32, (BLOCK, BLOCK), 0)
    col = lax.broadcasted_iota(jnp.int32, (BLOCK, BLOCK), 1)
    causal = row >= col
    lane = lax.broadcasted_iota(jnp.int32, (BLOCK, LANES), 1)
    heads_per_group = gw // SSD_HEAD
    for g in range(SSD_GROUPS):
        bg = bm[:, g * SSD_STATE:(g + 1) * SSD_STATE]
        cg_b = cm[:, g * SSD_STATE:(g + 1) * SSD_STATE].astype(BF16)
        gmat = _dot_nt(cg_b, bg.astype(BF16))
        for pr in range(heads_per_group // 2):
            slab = slice(g * gw + pr * LANES, g * gw + (pr + 1) * LANES)
            xp = xdt_b[:, slab]
            pair = None
            for hh in range(2):
                h = g * heads_per_group + pr * 2 + hh
                diff = acs[:, h:h + 1] - acs_t[h:h + 1, :]
                w = (gmat * jnp.exp(jnp.where(causal, diff, -jnp.inf))).astype(BF16)
                keep = (lane < SSD_HEAD) if hh == 0 else (lane >= SSD_HEAD)
                part = _dot(w, jnp.where(keep, xp, jnp.zeros_like(xp)))
                pair = part if pair is None else pair + part
            y_scr[:, slab] = pair
        gsl = slice(g * gw, (g + 1) * gw)
        st = st_scr[g]
        y_scr[:, gsl] = y_scr[:, gsl] + _dot(cg_b, st.astype(BF16)) * e1x[:, gsl]
        xw = (xs[:, gsl] * w2x[:, gsl]).astype(BF16)
        st_scr[g] = st * e1x[BLOCK - 1:BLOCK, gsl] + _dot(bg.T.astype(BF16), xw)

    y = y_scr[...] + xs * dsk_ref[...]
    y = y * (z * _sigmoid(z))
    for g in range(SSD_GROUPS):
        gsl = slice(g * gw, (g + 1) * gw)
        yg = y[:, gsl]
        ms = jnp.mean(yg * yg, axis=-1, keepdims=True)
        o_ref[:, gsl] = (yg * lax.rsqrt(ms + EPS) * nw_ref[:, gsl]).astype(BF16)


def _ssd(proj, cw, cb, dtb, alog, dsk, nw, tri, ex, *, batch, seq, d_ssd):
    nc = seq // BLOCK
    m, n = proj.shape
    d_conv = d_ssd + 2 * SSD_GROUPS * SSD_STATE
    gw = d_ssd // SSD_GROUPS
    const = lambda b, c: (0, 0)
    return pl.pallas_call(
        functools.partial(_ssd_kernel, d_ssd=d_ssd),
        grid=(batch, nc),
        in_specs=[
            pl.BlockSpec((BLOCK, n), lambda b, c: (b * nc + c, 0)),
            pl.BlockSpec((CONV_W, d_conv), const),
            pl.BlockSpec((1, d_conv), const),
            pl.BlockSpec((1, LANES), const),
            pl.BlockSpec((1, LANES), const),
            pl.BlockSpec((1, d_ssd), const),
            pl.BlockSpec((1, d_ssd), const),
            pl.BlockSpec((BLOCK, BLOCK), const),
            pl.BlockSpec((2 * LANES, d_ssd), const),
        ],
        out_specs=pl.BlockSpec((BLOCK, d_ssd), lambda b, c: (b * nc + c, 0)),
        out_shape=jax.ShapeDtypeStruct((m, d_ssd), BF16),
        scratch_shapes=[
            pltpu.VMEM((8 + BLOCK, d_conv), F32),
            pltpu.VMEM((SSD_GROUPS, SSD_STATE, gw), F32),
            pltpu.VMEM((BLOCK, d_ssd), F32),
        ],
        compiler_params=pltpu.CompilerParams(
            dimension_semantics=("arbitrary", "arbitrary"), vmem_limit_bytes=VMEM_LIMIT),
        name="ssd",
    )(proj, cw, cb, dtb, alog, dsk, nw, tri, ex)


def _outproj_kernel(a_ref, y_ref, x_ref, w_ref, o_ref):
    da = a_ref.shape[1]
    o_ref[...] = x_ref[...] + _dot(a_ref[...], w_ref[0:da, :]) + _dot(y_ref[...], w_ref[da:, :])


def _outproj(attn, y, x2, w_out, *, tm):
    m, d = x2.shape
    return pl.pallas_call(
        _outproj_kernel,
        grid=(m // tm,),
        in_specs=[
            pl.BlockSpec((tm, attn.shape[1]), lambda i: (i, 0)),
            pl.BlockSpec((tm, y.shape[1]), lambda i: (i, 0)),
            pl.BlockSpec((tm, d), lambda i: (i, 0)),
            pl.BlockSpec(w_out.shape, lambda i: (0, 0)),
        ],
        out_specs=pl.BlockSpec((tm, d), lambda i: (i, 0)),
        out_shape=jax.ShapeDtypeStruct((m, d), F32),
        compiler_params=pltpu.CompilerParams(
            dimension_semantics=("arbitrary",), vmem_limit_bytes=VMEM_LIMIT),
        name="outproj",
    )(attn, y, x2, w_out)


def _row_tile(m, want):
    t = want
    while m % t:
        t //= 2
    return t


def _layer(x2, ln_w, w_in, q_norm_w, k_norm_w, conv_w, conv_b, dt_bias, a_log, d_skip, ssd_norm_w, w_out,
           tables, *, batch, seq):
    d = x2.shape[1]
    d_ssd = w_out.shape[0] - D_ATTN
    n_ssd_heads = d_ssd // SSD_HEAD
    d_conv = d_ssd + 2 * SSD_GROUPS * SSD_STATE
    alibi, tri, ex = tables
    m = x2.shape[0]

    wq, wk, wv, wg = (w_in[:, k * D_ATTN:(k + 1) * D_ATTN] for k in range(4))
    w_kg = jnp.concatenate([wk, wg], axis=1).astype(BF16)
    w_t = jnp.stack([wq.T, wv.T]).astype(BF16)
    qwb = jnp.broadcast_to((q_norm_w * (HEAD_DIM ** -0.5 * LOG2E))[:, None], (HEAD_DIM, BLOCK))
    w_rest = w_in[:, 4 * D_ATTN:]
    pad = (-w_rest.shape[1]) % LANES
    w_ssd = jnp.pad(w_rest, ((0, 0), (0, pad))).astype(BF16)
    lnw = ln_w.reshape(1, d)

    def lane_pad(v):
        return jnp.pad(v, (0, LANES - v.shape[0])).reshape(1, LANES)

    kg, qt, vt, kmean = _inproj_attn(x2, lnw, w_kg, w_t, k_norm_w.reshape(1, HEAD_DIM), qwb,
                                     tm=_row_tile(m, 1024))
    attn = _moba_attn(kg, qt, vt, kmean, alibi, batch=batch, seq=seq, hg=ATTN_HEADS_PER_STEP)
    proj = _inproj_ssd(x2, lnw, w_ssd, tm=_row_tile(m, 512))
    y = _ssd(proj, conv_w, conv_b.reshape(1, d_conv), lane_pad(dt_bias), lane_pad(a_log),
             jnp.repeat(d_skip, SSD_HEAD).reshape(1, d_ssd), ssd_norm_w.reshape(1, d_ssd), tri, ex,
             batch=batch, seq=seq, d_ssd=d_ssd)
    del n_ssd_heads
    return _outproj(attn, y, x2, w_out.astype(BF16), tm=_row_tile(m, 512))


def kernel(x, ln_w, w_in, q_norm_w, k_norm_w, conv_w, conv_b, dt_bias, a_log, d_skip, ssd_norm_w, w_out):
    batch, seq, d = x.shape
    assert seq % BLOCK == 0 and d == D_ATTN
    d_ssd = w_out.shape[1] - D_ATTN
    n_heads = d_ssd // SSD_HEAD
    alibi = _alibi_tables()
    tri = jnp.asarray(np.tril(np.ones((BLOCK, BLOCK), np.float32))).astype(BF16)
    ex_np = np.zeros((LANES, d_ssd), np.float32)
    ex_np[np.repeat(np.arange(n_heads), SSD_HEAD), np.arange(d_ssd)] = 1.0
    ex = jnp.asarray(np.concatenate([ex_np, ex_np], axis=0)).astype(BF16)
    tables = (alibi, tri, ex)
    h = x.reshape(batch * seq, d)
    for i in range(ln_w.shape[0]):
        h = _layer(h, ln_w[i], w_in[i], q_norm_w[i], k_norm_w[i], conv_w[i], conv_b[i], dt_bias[i],
                   a_log[i], d_skip[i], ssd_norm_w[i], w_out[i], tables, batch=batch, seq=seq)
    return h.reshape(batch, seq, d)
```

```python
import functools

import jax
import jax.numpy as jnp
import numpy as np
from jax import lax
from jax.experimental import pallas as pl
from jax.experimental.pallas import tpu as pltpu

F32 = jnp.float32
BF16 = jnp.bfloat16

HEAD_DIM = 128
N_HEADS = 8
D_ATTN = N_HEADS * HEAD_DIM
BLOCK = 256
TOPK = 3
SSD_HEAD = 64
SSD_STATE = 128
SSD_GROUPS = 2
CONV_W = 4
EPS = 1e-6
NEG = -1e30
LOG2E = 1.4426950408889634
ATTN_HEADS_PER_STEP = 8
LANES = 128
MXU_COLS = 256
VMEM_LIMIT = 56 * 1024 * 1024


def _sigmoid(x):
    return 1.0 / (1.0 + jnp.exp(-x))


def _dot_nt(a, b):
    return lax.dot_general(a, b, (((1,), (1,)), ((), ())), preferred_element_type=F32)


def _dot(a, b):
    return jnp.dot(a, b, preferred_element_type=F32)


def _split2(x):
    hi = x.astype(BF16)
    lo = (x - hi.astype(F32)).astype(BF16)
    return hi, lo


def _inproj_attn_kernel(x_ref, lnw_ref, w_ref, wt_ref, kw_ref, qwb_ref, o_ref, qt_ref, vt_ref, km_ref, h_scr, *, tm):
    j = pl.program_id(1)
    d = D_ATTN
    nblk = tm // BLOCK

    @pl.when(j == 0)
    def _():
        x = x_ref[...]
        ms = jnp.mean(x * x, axis=-1, keepdims=True)
        h_scr[...] = (x * lax.rsqrt(ms + EPS) * lnw_ref[...]).astype(BF16)
        for c in range(d // 256):
            acc = _dot(h_scr[...], w_ref[:, c * 256:(c + 1) * 256])
            for hh in range(2):
                lo = c * 256 + hh * HEAD_DIM
                a = acc[:, hh * HEAD_DIM:(hh + 1) * HEAD_DIM]
                y = a * lax.rsqrt(jnp.mean(a * a, axis=-1, keepdims=True) + EPS) * kw_ref[...]
                o_ref[:, lo:lo + HEAD_DIM] = y.astype(BF16)
                for b in range(nblk):
                    km_ref[0, b:b + 1, lo:lo + HEAD_DIM] = jnp.mean(
                        y[b * BLOCK:(b + 1) * BLOCK], axis=0, keepdims=True)

    @pl.when(j == 1)
    def _():
        for c in range(d // 256):
            acc = _dot(h_scr[...], w_ref[:, c * 256:(c + 1) * 256])
            o_ref[:, c * 256:(c + 1) * 256] = (acc * _sigmoid(acc)).astype(BF16)

    @pl.when(j == 2)
    def _():
        for b in range(nblk):
            a = _dot_nt(wt_ref[0], h_scr[b * BLOCK:(b + 1) * BLOCK, :])
            for hh in range(N_HEADS):
                hs = slice(hh * HEAD_DIM, (hh + 1) * HEAD_DIM)
                ah = a[hs, :]
                y = ah * lax.rsqrt(jnp.mean(ah * ah, axis=0, keepdims=True) + EPS) * qwb_ref[...]
                qt_ref[b, hs, :] = y.astype(BF16)

    @pl.when(j == 3)
    def _():
        for b in range(nblk):
            vt_ref[b] = _dot_nt(wt_ref[0], h_scr[b * BLOCK:(b + 1) * BLOCK, :]).astype(BF16)


def _inproj_attn(x2, ln_w, w_kg, w_t, kw, qwb, *, tm):
    m, d = x2.shape
    nblk = tm // BLOCK
    grid = (m // tm, 4)
    t_spec = pl.BlockSpec((nblk, D_ATTN, BLOCK), lambda i, j: (i, 0, 0))
    t_shape = jax.ShapeDtypeStruct((m // BLOCK, D_ATTN, BLOCK), BF16)
    return pl.pallas_call(
        functools.partial(_inproj_attn_kernel, tm=tm),
        grid=grid,
        in_specs=[
            pl.BlockSpec((tm, d), lambda i, j: (i, 0)),
            pl.BlockSpec((1, d), lambda i, j: (0, 0)),
            pl.BlockSpec((d, D_ATTN), lambda i, j: (0, jnp.minimum(j, 1))),
            pl.BlockSpec((1, D_ATTN, d), lambda i, j: (jnp.maximum(j - 2, 0), 0, 0)),
            pl.BlockSpec((1, HEAD_DIM), lambda i, j: (0, 0)),
            pl.BlockSpec((HEAD_DIM, BLOCK), lambda i, j: (0, 0)),
        ],
        out_specs=[
            pl.BlockSpec((tm, D_ATTN), lambda i, j: (i, jnp.minimum(j, 1))),
            t_spec,
            t_spec,
            pl.BlockSpec((1, nblk, D_ATTN), lambda i, j: (i, 0, 0)),
        ],
        out_shape=[
            jax.ShapeDtypeStruct((m, 2 * D_ATTN), BF16),
            t_shape,
            t_shape,
            jax.ShapeDtypeStruct((m // tm, nblk, D_ATTN), F32),
        ],
        scratch_shapes=[pltpu.VMEM((tm, d), BF16)],
        compiler_params=pltpu.CompilerParams(
            dimension_semantics=("arbitrary", "arbitrary"), vmem_limit_bytes=VMEM_LIMIT),
        name="inproj_attn",
    )(x2, ln_w, w_kg, w_t, kw, qwb)


def _attn_kernel(qt_ref, k_ref, vt_ref, g_ref, km_ref, acol_ref, adiag_ref, srow_ref, o_ref,
                 off_scr, mx_scr, m_scr, a_scr, l_scr, acc_scr, *sp_scrs, nb, hg):
    s_scrs, p_scrs = sp_scrs[:hg], sp_scrs[hg:]
    i = pl.program_id(2)
    blk = lax.broadcasted_iota(jnp.int32, (nb, BLOCK), 0)
    blk_f = blk.astype(F32)
    elig = blk < i
    ones16 = jnp.ones((16, BLOCK), BF16)

    def scores(hh, k_blk, causal):
        hs = slice(hh * HEAD_DIM, (hh + 1) * HEAD_DIM)
        if causal:
            bias = adiag_ref[hh]
        else:
            bias = jnp.concatenate([acol_ref[hh]] * (BLOCK // LANES), axis=1)
        s = _dot(k_ref[k_blk, :, hs], qt_ref[hs, :]) + bias
        s_scrs[hh][...] = s
        mx_scr[hh] = jnp.max(s, axis=0, keepdims=True)

    for hh in range(hg):
        hs = slice(hh * HEAD_DIM, (hh + 1) * HEAD_DIM)
        qt = qt_ref[hs, :]
        km_hi, km_lo = _split2(km_ref[:, hs])
        gate = _dot(km_hi, qt) + _dot(km_lo, qt)
        left = elig
        for _ in range(TOPK):
            g_left = jnp.where(left, gate, -jnp.inf)
            top = (g_left == jnp.max(g_left, axis=0, keepdims=True)) & left
            first = jnp.min(jnp.where(top, blk_f, float(nb)), axis=0, keepdims=True)
            left = left & (blk_f != first)
        sel = elig & jnp.logical_not(left)
        off = srow_ref[hh] * ((blk - i) * BLOCK).astype(F32) + jnp.where(sel, 0.0, NEG)
        off_scr[hh] = jnp.where(blk == 0, 0.0, pltpu.roll(off, 1, axis=0))
        scores(hh, i, 1)
        m_scr[hh] = jnp.full((1, BLOCK), NEG, F32)
        a_scr[hh] = jnp.ones((1, BLOCK), F32)
        l_scr[hh] = jnp.zeros((8, BLOCK), F32)
        acc_scr[hh] = jnp.zeros((HEAD_DIM, BLOCK), F32)
        p_scrs[hh][...] = jnp.zeros((BLOCK, BLOCK), BF16)

    def apply_prev(hh, v_blk):
        hs = slice(hh * HEAD_DIM, (hh + 1) * HEAD_DIM)
        p_prev = p_scrs[hh][...]
        a_prev = a_scr[hh]
        pv = _dot(jnp.concatenate([vt_ref[v_blk, hs, :], ones16], axis=0), p_prev)
        acc_scr[hh] = a_prev * acc_scr[hh] + pv[0:HEAD_DIM]
        l_scr[hh] = a_prev * l_scr[hh] + pv[HEAD_DIM:HEAD_DIM + 8]

    def body(j, _):
        v_prev = jnp.where(j == 1, i, jnp.maximum(j - 2, 0))
        for hh in range(hg):
            apply_prev(hh, v_prev)
            off = off_scr[hh, pl.ds(j, 1), :]
            m_old = m_scr[hh]
            m_new = jnp.maximum(m_old, mx_scr[hh] + off)
            p_scrs[hh][...] = jnp.exp2(s_scrs[hh][...] - (m_new - off)).astype(BF16)
            a_scr[hh] = jnp.exp2(m_old - m_new)
            m_scr[hh] = m_new
            scores(hh, j, 0)
        return 0

    lax.fori_loop(0, i + 1, body, 0)
    v_last = jnp.maximum(i - 1, 0)
    for hh in range(hg):
        hs = slice(hh * HEAD_DIM, (hh + 1) * HEAD_DIM)
        apply_prev(hh, v_last)
        out = (acc_scr[hh] / l_scr[hh, 0:1, :]).T
        o_ref[:, hs] = (out * g_ref[:, hs].astype(F32)).astype(BF16)


def _moba_attn(kg, qt, vt, kmean, alibi, *, batch, seq, hg):
    nb = seq // BLOCK
    m = batch * seq
    hw = hg * HEAD_DIM
    ng = N_HEADS // hg
    kg3 = kg.reshape(m // BLOCK, BLOCK, 2 * D_ATTN)
    km2 = kmean.reshape(m // BLOCK, D_ATTN)
    return pl.pallas_call(
        functools.partial(_attn_kernel, nb=nb, hg=hg),
        grid=(batch, ng, nb),
        in_specs=[
            pl.BlockSpec((None, hw, BLOCK), lambda b, h, i: (b * nb + i, h, 0)),
            pl.BlockSpec((nb, BLOCK, hw), lambda b, h, i: (b, 0, h)),
            pl.BlockSpec((nb, hw, BLOCK), lambda b, h, i: (b, h, 0)),
            pl.BlockSpec((None, BLOCK, hw), lambda b, h, i: (b * nb + i, 0, ng + h)),
            pl.BlockSpec((nb, hw), lambda b, h, i: (b, h)),
            pl.BlockSpec((hg, BLOCK, LANES), lambda b, h, i: (h, 0, 0)),
            pl.BlockSpec((hg, BLOCK, BLOCK), lambda b, h, i: (h, 0, 0)),
            pl.BlockSpec((hg, 1, BLOCK), lambda b, h, i: (h, 0, 0)),
        ],
        out_specs=pl.BlockSpec((None, BLOCK, hw), lambda b, h, i: (b * nb + i, 0, h)),
        out_shape=jax.ShapeDtypeStruct((m // BLOCK, BLOCK, D_ATTN), BF16),
        scratch_shapes=[
            pltpu.VMEM((hg, nb, BLOCK), F32),
            pltpu.VMEM((hg, 1, BLOCK), F32),
            pltpu.VMEM((hg, 1, BLOCK), F32),
            pltpu.VMEM((hg, 1, BLOCK), F32),
            pltpu.VMEM((hg, 8, BLOCK), F32),
            pltpu.VMEM((hg, HEAD_DIM, BLOCK), F32),
        ] + [pltpu.VMEM((BLOCK, BLOCK), F32) for _ in range(hg)]
          + [pltpu.VMEM((BLOCK, BLOCK), BF16) for _ in range(hg)],
        compiler_params=pltpu.CompilerParams(
            dimension_semantics=("arbitrary", "arbitrary", "arbitrary"), vmem_limit_bytes=VMEM_LIMIT),
        name="moba_attn",
    )(qt, kg3, vt, kg3, km2, *alibi).reshape(m, D_ATTN)


def _alibi_tables():
    slopes = (LOG2E * np.exp2(-8.0 * np.arange(1, N_HEADS + 1, dtype=np.float64) / N_HEADS)).astype(np.float32)
    c = np.arange(BLOCK, dtype=np.float32)
    past = slopes[:, None, None] * np.broadcast_to(c[:, None], (BLOCK, BLOCK))[None]
    causal = np.where(c[:, None] <= c[None, :], 0.0, NEG).astype(np.float32)
    acol = past[:, :, :LANES].astype(np.float32)
    adiag = (past + causal[None]).astype(np.float32)
    srow = np.broadcast_to(slopes[:, None, None], (N_HEADS, 1, BLOCK)).astype(np.float32)
    return jnp.asarray(acol), jnp.asarray(adiag), jnp.asarray(srow)


def _ssd_chunk(p_ref, cw_ref, cb_ref, dtb_ref, alog_ref, dsk_ref, nw_ref, tri_ref, ex_ref,
               o_ref, xpad_scr, st_scr, y_scr, *, d_ssd):
    d_conv = d_ssd + 2 * SSD_GROUPS * SSD_STATE
    gw = d_ssd // SSD_GROUPS

    z = p_ref[:, 0:d_ssd]
    u = p_ref[:, d_ssd:d_ssd + d_conv]
    dtr = p_ref[:, d_ssd + d_conv:d_ssd + d_conv + LANES]

    xpad_scr[8:8 + BLOCK, :] = u
    conv = cb_ref[...] + cw_ref[CONV_W - 1:CONV_W, :] * u
    for s in range(1, CONV_W):
        conv = conv + cw_ref[CONV_W - 1 - s:CONV_W - s, :] * xpad_scr[8 - s:8 - s + BLOCK, :]
    xpad_scr[0:8, :] = u[BLOCK - 8:BLOCK, :]
    act = conv * _sigmoid(conv)
    xs = act[:, 0:d_ssd]
    bm = act[:, d_ssd:d_ssd + SSD_GROUPS * SSD_STATE]
    cm = act[:, d_ssd + SSD_GROUPS * SSD_STATE:d_conv]

    v = dtr + dtb_ref[...]
    dt = jnp.maximum(v, 0.0) + jnp.log1p(jnp.exp(-jnp.abs(v)))
    da = dt * (-jnp.exp(alog_ref[...]))
    hi = da.astype(BF16)
    r1 = da - hi.astype(F32)
    mid = r1.astype(BF16)
    lo = (r1 - mid.astype(F32)).astype(BF16)
    tri = tri_ref[...]
    acs = _dot(tri, hi) + _dot(tri, mid) + _dot(tri, lo)
    acs_t = acs.T
    e1 = jnp.exp(acs)
    e2 = jnp.exp(acs[BLOCK - 1:BLOCK, :] - acs)

    def expand(a):
        a_hi, a_lo = _split2(a)
        return _dot(jnp.concatenate([a_hi, a_lo], axis=1), ex_ref[...])

    dtx = expand(dt)
    e1x = expand(e1)
    w2x = expand(dt * e2)
    xdt_b = (xs * dtx).astype(BF16)

    row = lax.broadcasted_iota(jnp.int32, (BLOCK, BLOCK), 0)
    col = lax.broadcasted_iota(jnp.int32, (BLOCK, BLOCK), 1)
    causal = row >= col
    lane = lax.broadcasted_iota(jnp.int32, (BLOCK, LANES), 1)
    heads_per_group = gw // SSD_HEAD
    for g in range(SSD_GROUPS):
        bg = bm[:, g * SSD_STATE:(g + 1) * SSD_STATE]
        cg_b = cm[:, g * SSD_STATE:(g + 1) * SSD_STATE].astype(BF16)
        gmat = _dot_nt(cg_b, bg.astype(BF16))
        for pr in range(heads_per_group // 2):
            slab = slice(g * gw + pr * LANES, g * gw + (pr + 1) * LANES)
            xp = xdt_b[:, slab]
            pair = None
            for hh in range(2):
                h = g * heads_per_group + pr * 2 + hh
                diff = acs[:, h:h + 1] - acs_t[h:h + 1, :]
                w = (gmat * jnp.exp(jnp.where(causal, diff, -jnp.inf))).astype(BF16)
                keep = (lane < SSD_HEAD) if hh == 0 else (lane >= SSD_HEAD)
                part = _dot(w, jnp.where(keep, xp, jnp.zeros_like(xp)))
                pair = part if pair is None else pair + part
            y_scr[:, slab] = pair
        gsl = slice(g * gw, (g + 1) * gw)
        st = st_scr[g]
        y_scr[:, gsl] = y_scr[:, gsl] + _dot(cg_b, st.astype(BF16)) * e1x[:, gsl]
        xw = (xs[:, gsl] * w2x[:, gsl]).astype(BF16)
        st_scr[g] = st * e1x[BLOCK - 1:BLOCK, gsl] + _dot(bg.T.astype(BF16), xw)

    y = y_scr[...] + xs * dsk_ref[...]
    y = y * (z * _sigmoid(z))
    for g in range(SSD_GROUPS):
        gsl = slice(g * gw, (g + 1) * gw)
        yg = y[:, gsl]
        ms = jnp.mean(yg * yg, axis=-1, keepdims=True)
        o_ref[:, gsl] = (yg * lax.rsqrt(ms + EPS) * nw_ref[:, gsl]).astype(BF16)


def _ssd_out_kernel(x_ref, lnw_ref, w_ref, cw_ref, cb_ref, dtb_ref, alog_ref, dsk_ref, nw_ref, tri_ref, ex_ref,
                    attn_ref, wout_ref, o_ref, proj_scr, yb_scr, xpad_scr, st_scr, y_scr, *, d_ssd, cps):
    c = pl.program_id(1)
    d_conv = d_ssd + 2 * SSD_GROUPS * SSD_STATE

    @pl.when(c == 0)
    def _():
        xpad_scr[0:8, :] = jnp.zeros((8, d_conv), F32)
        st_scr[...] = jnp.zeros_like(st_scr)

    x = x_ref[...]
    ms = jnp.mean(x * x, axis=-1, keepdims=True)
    h = (x * lax.rsqrt(ms + EPS) * lnw_ref[...]).astype(BF16)
    n = w_ref.shape[1]
    step = 2 * MXU_COLS
    for lo in range(0, n, step):
        sl = slice(lo, min(lo + step, n))
        proj_scr[:, sl] = _dot(h, w_ref[:, sl])
    for ck in range(cps):
        rows = pl.ds(ck * BLOCK, BLOCK)
        _ssd_chunk(proj_scr.at[rows], cw_ref, cb_ref, dtb_ref, alog_ref, dsk_ref, nw_ref, tri_ref, ex_ref,
                   yb_scr.at[rows], xpad_scr, st_scr, y_scr, d_ssd=d_ssd)
    da = attn_ref.shape[1]
    o_ref[...] = x_ref[...] + _dot(attn_ref[...], wout_ref[0:da, :]) + _dot(yb_scr[...], wout_ref[da:, :])


def _ssd_out(x2, ln_w, w_ssd, cw, cb, dtb, alog, dsk, nw, tri, ex, attn, w_out, *, batch, seq, d_ssd):
    nc = seq // BLOCK
    cps = 2 if nc % 2 == 0 else 1
    rows = cps * BLOCK
    steps = nc // cps
    m, d = x2.shape
    n = w_ssd.shape[1]
    d_conv = d_ssd + 2 * SSD_GROUPS * SSD_STATE
    gw = d_ssd // SSD_GROUPS
    const = lambda b, c: (0, 0)
    tile = lambda b, c: (b * steps + c, 0)
    return pl.pallas_call(
        functools.partial(_ssd_out_kernel, d_ssd=d_ssd, cps=cps),
        grid=(batch, steps),
        in_specs=[
            pl.BlockSpec((rows, d), tile),
            pl.BlockSpec((1, d), const),
            pl.BlockSpec((d, n), const),
            pl.BlockSpec((CONV_W, d_conv), const),
            pl.BlockSpec((1, d_conv), const),
            pl.BlockSpec((1, LANES), const),
            pl.BlockSpec((1, LANES), const),
            pl.BlockSpec((1, d_ssd), const),
            pl.BlockSpec((1, d_ssd), const),
            pl.BlockSpec((BLOCK, BLOCK), const),
            pl.BlockSpec((2 * LANES, d_ssd), const),
            pl.BlockSpec((rows, attn.shape[1]), tile),
            pl.BlockSpec(w_out.shape, const),
        ],
        out_specs=pl.BlockSpec((rows, d), tile),
        out_shape=jax.ShapeDtypeStruct((m, d), F32),
        scratch_shapes=[
            pltpu.VMEM((rows, n), F32),
            pltpu.VMEM((rows, d_ssd), BF16),
            pltpu.VMEM((8 + BLOCK, d_conv), F32),
            pltpu.VMEM((SSD_GROUPS, SSD_STATE, gw), F32),
            pltpu.VMEM((BLOCK, d_ssd), F32),
        ],
        compiler_params=pltpu.CompilerParams(
            dimension_semantics=("arbitrary", "arbitrary"), vmem_limit_bytes=VMEM_LIMIT),
        name="ssd_out",
    )(x2, ln_w, w_ssd, cw, cb, dtb, alog, dsk, nw, tri, ex, attn, w_out)


def _row_tile(m, want):
    t = want
    while m % t:
        t //= 2
    return t


def _layer(x2, ln_w, w_in, q_norm_w, k_norm_w, conv_w, conv_b, dt_bias, a_log, d_skip, ssd_norm_w, w_out,
           tables, *, batch, seq):
    d = x2.shape[1]
    d_ssd = w_out.shape[0] - D_ATTN
    d_conv = d_ssd + 2 * SSD_GROUPS * SSD_STATE
    alibi, tri, ex = tables
    m = x2.shape[0]

    wq, wk, wv, wg = (w_in[:, k * D_ATTN:(k + 1) * D_ATTN] for k in range(4))
    w_kg = jnp.concatenate([wk, wg], axis=1).astype(BF16)
    w_t = jnp.stack([wq.T, wv.T]).astype(BF16)
    qwb = jnp.broadcast_to((q_norm_w * (HEAD_DIM ** -0.5 * LOG2E))[:, None], (HEAD_DIM, BLOCK))
    w_rest = w_in[:, 4 * D_ATTN:]
    pad = (-w_rest.shape[1]) % LANES
    w_ssd = jnp.pad(w_rest, ((0, 0), (0, pad))).astype(BF16)
    lnw = ln_w.reshape(1, d)

    def lane_pad(v):
        return jnp.pad(v, (0, LANES - v.shape[0])).reshape(1, LANES)

    kg, qt, vt, kmean = _inproj_attn(x2, lnw, w_kg, w_t, k_norm_w.reshape(1, HEAD_DIM), qwb,
                                     tm=_row_tile(m, 1024))
    attn = _moba_attn(kg, qt, vt, kmean, alibi, batch=batch, seq=seq, hg=ATTN_HEADS_PER_STEP)
    return _ssd_out(x2, lnw, w_ssd, conv_w, conv_b.reshape(1, d_conv), lane_pad(dt_bias), lane_pad(a_log),
                    jnp.repeat(d_skip, SSD_HEAD).reshape(1, d_ssd), ssd_norm_w.reshape(1, d_ssd), tri, ex,
                    attn, w_out.astype(BF16), batch=batch, seq=seq, d_ssd=d_ssd)


def kernel(x, ln_w, w_in, q_norm_w, k_norm_w, conv_w, conv_b, dt_bias, a_log, d_skip, ssd_norm_w, w_out):
    batch, seq, d = x.shape
    assert seq % BLOCK == 0 and d == D_ATTN
    d_ssd = w_out.shape[1] - D_ATTN
    n_heads = d_ssd // SSD_HEAD
    alibi = _alibi_tables()
    tri = jnp.asarray(np.tril(np.ones((BLOCK, BLOCK), np.float32))).astype(BF16)
    ex_np = np.zeros((LANES, d_ssd), np.float32)
    ex_np[np.repeat(np.arange(n_heads), SSD_HEAD), np.arange(d_ssd)] = 1.0
    ex = jnp.asarray(np.concatenate([ex_np, ex_np], axis=0)).astype(BF16)
    tables = (alibi, tri, ex)
    h = x.reshape(batch * seq, d)
    for i in range(ln_w.shape[0]):
        h = _layer(h, ln_w[i], w_in[i], q_norm_w[i], k_norm_w[i], conv_w[i], conv_b[i], dt_bias[i],
                   a_log[i], d_skip[i], ssd_norm_w[i], w_out[i], tables, batch=batch, seq=seq)
    return h.reshape(batch, seq, d)
```

```python
import functools

import jax
import jax.numpy as jnp
import numpy as np
from jax import lax
from jax.experimental import pallas as pl
from jax.experimental.pallas import tpu as pltpu

F32 = jnp.float32
BF16 = jnp.bfloat16

HEAD_DIM = 128
N_HEADS = 8
D_ATTN = N_HEADS * HEAD_DIM
BLOCK = 256
TOPK = 3
SSD_HEAD = 64
SSD_STATE = 128
SSD_GROUPS = 2
CONV_W = 4
EPS = 1e-6
NEG = -1e30
LOG2E = 1.4426950408889634
ATTN_HEADS_PER_STEP = 8
LANES = 128
MXU_COLS = 256
VMEM_LIMIT = 56 * 1024 * 1024


def _sigmoid(x):
    return 1.0 / (1.0 + jnp.exp(-x))


def _dot_nt(a, b):
    return lax.dot_general(a, b, (((1,), (1,)), ((), ())), preferred_element_type=F32)


def _dot(a, b):
    return jnp.dot(a, b, preferred_element_type=F32)


def _split2(x):
    hi = x.astype(BF16)
    lo = (x - hi.astype(F32)).astype(BF16)
    return hi, lo


def _inproj_attn_kernel(x_ref, lnw_ref, w_ref, wt_ref, kw_ref, qwb_ref, o_ref, qt_ref, vt_ref, km_ref, h_scr, *, tm):
    j = pl.program_id(1)
    d = D_ATTN
    nblk = tm // BLOCK

    @pl.when(j == 0)
    def _():
        x = x_ref[...]
        ms = jnp.mean(x * x, axis=-1, keepdims=True)
        h_scr[...] = (x * lax.rsqrt(ms + EPS) * lnw_ref[...]).astype(BF16)
        for c in range(d // 256):
            acc = _dot(h_scr[...], w_ref[:, c * 256:(c + 1) * 256])
            for hh in range(2):
                lo = c * 256 + hh * HEAD_DIM
                a = acc[:, hh * HEAD_DIM:(hh + 1) * HEAD_DIM]
                y = a * lax.rsqrt(jnp.mean(a * a, axis=-1, keepdims=True) + EPS) * kw_ref[...]
                o_ref[:, lo:lo + HEAD_DIM] = y.astype(BF16)
                for b in range(nblk):
                    km_ref[0, b:b + 1, lo:lo + HEAD_DIM] = jnp.mean(
                        y[b * BLOCK:(b + 1) * BLOCK], axis=0, keepdims=True)

    @pl.when(j == 1)
    def _():
        for c in range(d // 256):
            acc = _dot(h_scr[...], w_ref[:, c * 256:(c + 1) * 256])
            o_ref[:, c * 256:(c + 1) * 256] = (acc * _sigmoid(acc)).astype(BF16)

    @pl.when(j == 2)
    def _():
        for b in range(nblk):
            a = _dot_nt(wt_ref[0], h_scr[b * BLOCK:(b + 1) * BLOCK, :])
            for hh in range(N_HEADS):
                hs = slice(hh * HEAD_DIM, (hh + 1) * HEAD_DIM)
                ah = a[hs, :]
                y = ah * lax.rsqrt(jnp.mean(ah * ah, axis=0, keepdims=True) + EPS) * qwb_ref[...]
                qt_ref[b, hs, :] = y.astype(BF16)

    @pl.when(j == 3)
    def _():
        for b in range(nblk):
            vt_ref[b] = _dot_nt(wt_ref[0], h_scr[b * BLOCK:(b + 1) * BLOCK, :]).astype(BF16)


def _inproj_attn(x2, ln_w, w_kg, w_t, kw, qwb, *, tm):
    m, d = x2.shape
    nblk = tm // BLOCK
    grid = (m // tm, 4)
    t_spec = pl.BlockSpec((nblk, D_ATTN, BLOCK), lambda i, j: (i, 0, 0))
    t_shape = jax.ShapeDtypeStruct((m // BLOCK, D_ATTN, BLOCK), BF16)
    return pl.pallas_call(
        functools.partial(_inproj_attn_kernel, tm=tm),
        grid=grid,
        in_specs=[
            pl.BlockSpec((tm, d), lambda i, j: (i, 0)),
            pl.BlockSpec((1, d), lambda i, j: (0, 0)),
            pl.BlockSpec((d, D_ATTN), lambda i, j: (0, jnp.minimum(j, 1))),
            pl.BlockSpec((1, D_ATTN, d), lambda i, j: (jnp.maximum(j - 2, 0), 0, 0)),
            pl.BlockSpec((1, HEAD_DIM), lambda i, j: (0, 0)),
            pl.BlockSpec((HEAD_DIM, BLOCK), lambda i, j: (0, 0)),
        ],
        out_specs=[
            pl.BlockSpec((tm, D_ATTN), lambda i, j: (i, jnp.minimum(j, 1))),
            t_spec,
            t_spec,
            pl.BlockSpec((1, nblk, D_ATTN), lambda i, j: (i, 0, 0)),
        ],
        out_shape=[
            jax.ShapeDtypeStruct((m, 2 * D_ATTN), BF16),
            t_shape,
            t_shape,
            jax.ShapeDtypeStruct((m // tm, nblk, D_ATTN), F32),
        ],
        scratch_shapes=[pltpu.VMEM((tm, d), BF16)],
        compiler_params=pltpu.CompilerParams(
            dimension_semantics=("arbitrary", "arbitrary"), vmem_limit_bytes=VMEM_LIMIT),
        name="inproj_attn",
    )(x2, ln_w, w_kg, w_t, kw, qwb)


def _attn_kernel(qt_ref, k_ref, vt_ref, g_ref, km_ref, acol_ref, adiag_ref, srow_ref, o_ref,
                 off_scr, mx_scr, m_scr, a_scr, l_scr, acc_scr, *sp_scrs, nb, hg):
    s_scrs, p_scrs = sp_scrs[:hg], sp_scrs[hg:]
    i = pl.program_id(2)
    blk = lax.broadcasted_iota(jnp.int32, (nb, BLOCK), 0)
    blk_f = blk.astype(F32)
    elig = blk < i
    ones16 = jnp.ones((16, BLOCK), BF16)

    def scores(hh, k_blk, causal):
        hs = slice(hh * HEAD_DIM, (hh + 1) * HEAD_DIM)
        if causal:
            bias = adiag_ref[hh]
        else:
            bias = jnp.concatenate([acol_ref[hh]] * (BLOCK // LANES), axis=1)
        s = _dot(k_ref[k_blk, :, hs], qt_ref[hs, :]) + bias
        s_scrs[hh][...] = s
        mx_scr[hh] = jnp.max(s, axis=0, keepdims=True)

    for hh in range(hg):
        hs = slice(hh * HEAD_DIM, (hh + 1) * HEAD_DIM)
        qt = qt_ref[hs, :]
        km_hi, km_lo = _split2(km_ref[:, hs])
        gate = _dot(km_hi, qt) + _dot(km_lo, qt)
        left = elig
        for _ in range(TOPK):
            g_left = jnp.where(left, gate, -jnp.inf)
            top = (g_left == jnp.max(g_left, axis=0, keepdims=True)) & left
            first = jnp.min(jnp.where(top, blk_f, float(nb)), axis=0, keepdims=True)
            left = left & (blk_f != first)
        sel = elig & jnp.logical_not(left)
        off = srow_ref[hh] * ((blk - i) * BLOCK).astype(F32) + jnp.where(sel, 0.0, NEG)
        off_scr[hh] = jnp.where(blk == 0, 0.0, pltpu.roll(off, 1, axis=0))
        scores(hh, i, 1)
        m_scr[hh] = jnp.full((1, BLOCK), NEG, F32)
        a_scr[hh] = jnp.ones((1, BLOCK), F32)
        l_scr[hh] = jnp.zeros((8, BLOCK), F32)
        acc_scr[hh] = jnp.zeros((HEAD_DIM, BLOCK), F32)
        p_scrs[hh][...] = jnp.zeros((BLOCK, BLOCK), BF16)

    def pv_ones(hh, v_blk, p):
        hs = slice(hh * HEAD_DIM, (hh + 1) * HEAD_DIM)
        return _dot(jnp.concatenate([vt_ref[v_blk, hs, :], ones16], axis=0), p)

    def fold_prev(hh, v_blk):
        a_prev = a_scr[hh]
        pv = pv_ones(hh, v_blk, p_scrs[hh][...])
        return a_prev * acc_scr[hh] + pv[0:HEAD_DIM], a_prev * l_scr[hh] + pv[HEAD_DIM:HEAD_DIM + 8]

    def softmax_step(hh, j):
        off = off_scr[hh, pl.ds(j, 1), :]
        m_old = m_scr[hh]
        m_new = jnp.maximum(m_old, mx_scr[hh] + off)
        p = jnp.exp2(s_scrs[hh][...] - (m_new - off)).astype(BF16)
        return p, jnp.exp2(m_old - m_new), m_new

    def block_of_step(j):
        return jnp.where(j == 0, i, jnp.maximum(j - 1, 0))

    def body(j, _):
        v_prev = block_of_step(jnp.maximum(j - 1, 0))
        for hh in range(hg):
            acc_scr[hh], l_scr[hh] = fold_prev(hh, v_prev)
            p_scrs[hh][...], a_scr[hh], m_scr[hh] = softmax_step(hh, j)
            scores(hh, j, 0)
        return 0

    lax.fori_loop(0, i, body, 0)
    v_prev = block_of_step(jnp.maximum(i - 1, 0))
    v_last = block_of_step(i)
    for hh in range(hg):
        hs = slice(hh * HEAD_DIM, (hh + 1) * HEAD_DIM)
        acc, l = fold_prev(hh, v_prev)
        p, alpha, _ = softmax_step(hh, i)
        pv = pv_ones(hh, v_last, p)
        acc = alpha * acc + pv[0:HEAD_DIM]
        l = alpha * l + pv[HEAD_DIM:HEAD_DIM + 8]
        out = (acc / l[0:1, :]).T
        o_ref[:, hs] = (out * g_ref[:, hs].astype(F32)).astype(BF16)


def _moba_attn(kg, qt, vt, kmean, alibi, *, batch, seq, hg):
    nb = seq // BLOCK
    m = batch * seq
    hw = hg * HEAD_DIM
    ng = N_HEADS // hg
    kg3 = kg.reshape(m // BLOCK, BLOCK, 2 * D_ATTN)
    km2 = kmean.reshape(m // BLOCK, D_ATTN)
    return pl.pallas_call(
        functools.partial(_attn_kernel, nb=nb, hg=hg),
        grid=(batch, ng, nb),
        in_specs=[
            pl.BlockSpec((None, hw, BLOCK), lambda b, h, i: (b * nb + i, h, 0)),
            pl.BlockSpec((nb, BLOCK, hw), lambda b, h, i: (b, 0, h)),
            pl.BlockSpec((nb, hw, BLOCK), lambda b, h, i: (b, h, 0)),
            pl.BlockSpec((None, BLOCK, hw), lambda b, h, i: (b * nb + i, 0, ng + h)),
            pl.BlockSpec((nb, hw), lambda b, h, i: (b, h)),
            pl.BlockSpec((hg, BLOCK, LANES), lambda b, h, i: (h, 0, 0)),
            pl.BlockSpec((hg, BLOCK, BLOCK), lambda b, h, i: (h, 0, 0)),
            pl.BlockSpec((hg, 1, BLOCK), lambda b, h, i: (h, 0, 0)),
        ],
        out_specs=pl.BlockSpec((None, BLOCK, hw), lambda b, h, i: (b * nb + i, 0, h)),
        out_shape=jax.ShapeDtypeStruct((m // BLOCK, BLOCK, D_ATTN), BF16),
        scratch_shapes=[
            pltpu.VMEM((hg, nb, BLOCK), F32),
            pltpu.VMEM((hg, 1, BLOCK), F32),
            pltpu.VMEM((hg, 1, BLOCK), F32),
            pltpu.VMEM((hg, 1, BLOCK), F32),
            pltpu.VMEM((hg, 8, BLOCK), F32),
            pltpu.VMEM((hg, HEAD_DIM, BLOCK), F32),
        ] + [pltpu.VMEM((BLOCK, BLOCK), F32) for _ in range(hg)]
          + [pltpu.VMEM((BLOCK, BLOCK), BF16) for _ in range(hg)],
        compiler_params=pltpu.CompilerParams(
            dimension_semantics=("arbitrary", "arbitrary", "arbitrary"), vmem_limit_bytes=VMEM_LIMIT),
        name="moba_attn",
    )(qt, kg3, vt, kg3, km2, *alibi).reshape(m, D_ATTN)


def _alibi_tables():
    slopes = (LOG2E * np.exp2(-8.0 * np.arange(1, N_HEADS + 1, dtype=np.float64) / N_HEADS)).astype(np.float32)
    c = np.arange(BLOCK, dtype=np.float32)
    past = slopes[:, None, None] * np.broadcast_to(c[:, None], (BLOCK, BLOCK))[None]
    causal = np.where(c[:, None] <= c[None, :], 0.0, NEG).astype(np.float32)
    acol = past[:, :, :LANES].astype(np.float32)
    adiag = (past + causal[None]).astype(np.float32)
    srow = np.broadcast_to(slopes[:, None, None], (N_HEADS, 1, BLOCK)).astype(np.float32)
    return jnp.asarray(acol), jnp.asarray(adiag), jnp.asarray(srow)


def _ssd_chunk(p_ref, cw_ref, cb_ref, dtb_ref, alog_ref, dsk_ref, nw_ref, tri_ref, ex_ref,
               o_ref, xpad_scr, st_scr, y_scr, *, d_ssd):
    d_conv = d_ssd + 2 * SSD_GROUPS * SSD_STATE
    gw = d_ssd // SSD_GROUPS

    z = p_ref[:, 0:d_ssd]
    u = p_ref[:, d_ssd:d_ssd + d_conv]
    dtr = p_ref[:, d_ssd + d_conv:d_ssd + d_conv + LANES]

    xpad_scr[8:8 + BLOCK, :] = u
    conv = cb_ref[...] + cw_ref[CONV_W - 1:CONV_W, :] * u
    for s in range(1, CONV_W):
        conv = conv + cw_ref[CONV_W - 1 - s:CONV_W - s, :] * xpad_scr[8 - s:8 - s + BLOCK, :]
    xpad_scr[0:8, :] = u[BLOCK - 8:BLOCK, :]
    act = conv * _sigmoid(conv)
    xs = act[:, 0:d_ssd]
    bm = act[:, d_ssd:d_ssd + SSD_GROUPS * SSD_STATE]
    cm = act[:, d_ssd + SSD_GROUPS * SSD_STATE:d_conv]

    v = dtr + dtb_ref[...]
    dt = jnp.maximum(v, 0.0) + jnp.log1p(jnp.exp(-jnp.abs(v)))
    da = dt * (-jnp.exp(alog_ref[...]))
    hi = da.astype(BF16)
    r1 = da - hi.astype(F32)
    mid = r1.astype(BF16)
    lo = (r1 - mid.astype(F32)).astype(BF16)
    tri = tri_ref[...]
    acs = _dot(tri, hi) + _dot(tri, mid) + _dot(tri, lo)
    acs_t = acs.T
    e1 = jnp.exp(acs)
    e2 = jnp.exp(acs[BLOCK - 1:BLOCK, :] - acs)

    def expand(a):
        a_hi, a_lo = _split2(a)
        return _dot(jnp.concatenate([a_hi, a_lo], axis=1), ex_ref[...])

    dtx = expand(dt)
    e1x = expand(e1)
    w2x = expand(dt * e2)
    xdt_b = (xs * dtx).astype(BF16)

    row = lax.broadcasted_iota(jnp.int32, (BLOCK, BLOCK), 0)
    col = lax.broadcasted_iota(jnp.int32, (BLOCK, BLOCK), 1)
    causal = row >= col
    lane = lax.broadcasted_iota(jnp.int32, (BLOCK, LANES), 1)
    heads_per_group = gw // SSD_HEAD
    for g in range(SSD_GROUPS):
        bg = bm[:, g * SSD_STATE:(g + 1) * SSD_STATE]
        cg_b = cm[:, g * SSD_STATE:(g + 1) * SSD_STATE].astype(BF16)
        gmat = _dot_nt(cg_b, bg.astype(BF16))
        for pr in range(heads_per_group // 2):
            slab = slice(g * gw + pr * LANES, g * gw + (pr + 1) * LANES)
            xp = xdt_b[:, slab]
            pair = None
            for hh in range(2):
                h = g * heads_per_group + pr * 2 + hh
                diff = acs[:, h:h + 1] - acs_t[h:h + 1, :]
                w = (gmat * jnp.exp(jnp.where(causal, diff, -jnp.inf))).astype(BF16)
                keep = (lane < SSD_HEAD) if hh == 0 else (lane >= SSD_HEAD)
                part = _dot(w, jnp.where(keep, xp, jnp.zeros_like(xp)))
                pair = part if pair is None else pair + part
            y_scr[:, slab] = pair
        gsl = slice(g * gw, (g + 1) * gw)
        st = st_scr[g]
        y_scr[:, gsl] = y_scr[:, gsl] + _dot(cg_b, st.astype(BF16)) * e1x[:, gsl]
        xw = (xs[:, gsl] * w2x[:, gsl]).astype(BF16)
        st_scr[g] = st * e1x[BLOCK - 1:BLOCK, gsl] + _dot(bg.T.astype(BF16), xw)

    y = y_scr[...] + xs * dsk_ref[...]
    y = y * (z * _sigmoid(z))
    for g in range(SSD_GROUPS):
        gsl = slice(g * gw, (g + 1) * gw)
        yg = y[:, gsl]
        ms = jnp.mean(yg * yg, axis=-1, keepdims=True)
        o_ref[:, gsl] = (yg * lax.rsqrt(ms + EPS) * nw_ref[:, gsl]).astype(BF16)


def _ssd_out_kernel(x_ref, lnw_ref, w_ref, cw_ref, cb_ref, dtb_ref, alog_ref, dsk_ref, nw_ref, tri_ref, ex_ref,
                    attn_ref, wout_ref, o_ref, proj_scr, yb_scr, xpad_scr, st_scr, y_scr, *, d_ssd, cps):
    c = pl.program_id(1)
    d_conv = d_ssd + 2 * SSD_GROUPS * SSD_STATE

    @pl.when(c == 0)
    def _():
        xpad_scr[0:8, :] = jnp.zeros((8, d_conv), F32)
        st_scr[...] = jnp.zeros_like(st_scr)

    x = x_ref[...]
    ms = jnp.mean(x * x, axis=-1, keepdims=True)
    h = (x * lax.rsqrt(ms + EPS) * lnw_ref[...]).astype(BF16)
    n = w_ref.shape[1]
    step = 2 * MXU_COLS
    for lo in range(0, n, step):
        sl = slice(lo, min(lo + step, n))
        proj_scr[:, sl] = _dot(h, w_ref[:, sl])
    for ck in range(cps):
        rows = pl.ds(ck * BLOCK, BLOCK)
        _ssd_chunk(proj_scr.at[rows], cw_ref, cb_ref, dtb_ref, alog_ref, dsk_ref, nw_ref, tri_ref, ex_ref,
                   yb_scr.at[rows], xpad_scr, st_scr, y_scr, d_ssd=d_ssd)
    da = attn_ref.shape[1]
    o_ref[...] = x_ref[...] + _dot(attn_ref[...], wout_ref[0:da, :]) + _dot(yb_scr[...], wout_ref[da:, :])


def _ssd_out(x2, ln_w, w_ssd, cw, cb, dtb, alog, dsk, nw, tri, ex, attn, w_out, *, batch, seq, d_ssd):
    nc = seq // BLOCK
    cps = 2 if nc % 2 == 0 else 1
    rows = cps * BLOCK
    steps = nc // cps
    m, d = x2.shape
    n = w_ssd.shape[1]
    d_conv = d_ssd + 2 * SSD_GROUPS * SSD_STATE
    gw = d_ssd // SSD_GROUPS
    const = lambda b, c: (0, 0)
    tile = lambda b, c: (b * steps + c, 0)
    return pl.pallas_call(
        functools.partial(_ssd_out_kernel, d_ssd=d_ssd, cps=cps),
        grid=(batch, steps),
        in_specs=[
            pl.BlockSpec((rows, d), tile),
            pl.BlockSpec((1, d), const),
            pl.BlockSpec((d, n), const),
            pl.BlockSpec((CONV_W, d_conv), const),
            pl.BlockSpec((1, d_conv), const),
            pl.BlockSpec((1, LANES), const),
            pl.BlockSpec((1, LANES), const),
            pl.BlockSpec((1, d_ssd), const),
            pl.BlockSpec((1, d_ssd), const),
            pl.BlockSpec((BLOCK, BLOCK), const),
            pl.BlockSpec((2 * LANES, d_ssd), const),
            pl.BlockSpec((rows, attn.shape[1]), tile),
            pl.BlockSpec(w_out.shape, const),
        ],
        out_specs=pl.BlockSpec((rows, d), tile),
        out_shape=jax.ShapeDtypeStruct((m, d), F32),
        scratch_shapes=[
            pltpu.VMEM((rows, n), F32),
            pltpu.VMEM((rows, d_ssd), BF16),
            pltpu.VMEM((8 + BLOCK, d_conv), F32),
            pltpu.VMEM((SSD_GROUPS, SSD_STATE, gw), F32),
            pltpu.VMEM((BLOCK, d_ssd), F32),
        ],
        compiler_params=pltpu.CompilerParams(
            dimension_semantics=("arbitrary", "arbitrary"), vmem_limit_bytes=VMEM_LIMIT),
        name="ssd_out",
    )(x2, ln_w, w_ssd, cw, cb, dtb, alog, dsk, nw, tri, ex, attn, w_out)


def _row_tile(m, want):
    t = want
    while m % t:
        t //= 2
    return t


def _layer(x2, ln_w, w_in, q_norm_w, k_norm_w, conv_w, conv_b, dt_bias, a_log, d_skip, ssd_norm_w, w_out,
           tables, *, batch, seq):
    d = x2.shape[1]
    d_ssd = w_out.shape[0] - D_ATTN
    d_conv = d_ssd + 2 * SSD_GROUPS * SSD_STATE
    alibi, tri, ex = tables
    m = x2.shape[0]

    wq, wk, wv, wg = (w_in[:, k * D_ATTN:(k + 1) * D_ATTN] for k in range(4))
    w_kg = jnp.concatenate([wk, wg], axis=1).astype(BF16)
    w_t = jnp.stack([wq.T, wv.T]).astype(BF16)
    qwb = jnp.broadcast_to((q_norm_w * (HEAD_DIM ** -0.5 * LOG2E))[:, None], (HEAD_DIM, BLOCK))
    w_rest = w_in[:, 4 * D_ATTN:]
    pad = (-w_rest.shape[1]) % LANES
    w_ssd = jnp.pad(w_rest, ((0, 0), (0, pad))).astype(BF16)
    lnw = ln_w.reshape(1, d)

    def lane_pad(v):
        return jnp.pad(v, (0, LANES - v.shape[0])).reshape(1, LANES)

    kg, qt, vt, kmean = _inproj_attn(x2, lnw, w_kg, w_t, k_norm_w.reshape(1, HEAD_DIM), qwb,
                                     tm=_row_tile(m, 1024))
    attn = _moba_attn(kg, qt, vt, kmean, alibi, batch=batch, seq=seq, hg=ATTN_HEADS_PER_STEP)
    return _ssd_out(x2, lnw, w_ssd, conv_w, conv_b.reshape(1, d_conv), lane_pad(dt_bias), lane_pad(a_log),
                    jnp.repeat(d_skip, SSD_HEAD).reshape(1, d_ssd), ssd_norm_w.reshape(1, d_ssd), tri, ex,
                    attn, w_out.astype(BF16), batch=batch, seq=seq, d_ssd=d_ssd)


def kernel(x, ln_w, w_in, q_norm_w, k_norm_w, conv_w, conv_b, dt_bias, a_log, d_skip, ssd_norm_w, w_out):
    batch, seq, d = x.shape
    assert seq % BLOCK == 0 and d == D_ATTN
    d_ssd = w_out.shape[1] - D_ATTN
    n_heads = d_ssd // SSD_HEAD
    alibi = _alibi_tables()
    tri = jnp.asarray(np.tril(np.ones((BLOCK, BLOCK), np.float32))).astype(BF16)
    ex_np = np.zeros((LANES, d_ssd), np.float32)
    ex_np[np.repeat(np.arange(n_heads), SSD_HEAD), np.arange(d_ssd)] = 1.0
    ex = jnp.asarray(np.concatenate([ex_np, ex_np], axis=0)).astype(BF16)
    tables = (alibi, tri, ex)
    h = x.reshape(batch * seq, d)
    for i in range(ln_w.shape[0]):
        h = _layer(h, ln_w[i], w_in[i], q_norm_w[i], k_norm_w[i], conv_w[i], conv_b[i], dt_bias[i],
                   a_log[i], d_skip[i], ssd_norm_w[i], w_out[i], tables, batch=batch, seq=seq)
    return h.reshape(batch, seq, d)
```

```python
import functools

import jax
import jax.numpy as jnp
import numpy as np
from jax import lax
from jax.experimental import pallas as pl
from jax.experimental.pallas import tpu as pltpu

F32 = jnp.float32
BF16 = jnp.bfloat16

HEAD_DIM = 128
N_HEADS = 8
D_ATTN = N_HEADS * HEAD_DIM
BLOCK = 256
TOPK = 3
SSD_HEAD = 64
SSD_STATE = 128
SSD_GROUPS = 2
CONV_W = 4
EPS = 1e-6
NEG = -1e30
LOG2E = 1.4426950408889634
ATTN_HEADS_PER_STEP = 8
LANES = 128
MXU_COLS = 256
VMEM_LIMIT = 56 * 1024 * 1024


def _sigmoid(x):
    return 1.0 / (1.0 + jnp.exp(-x))


def _dot_nt(a, b):
    return lax.dot_general(a, b, (((1,), (1,)), ((), ())), preferred_element_type=F32)


def _dot(a, b):
    return jnp.dot(a, b, preferred_element_type=F32)


def _split2(x):
    hi = x.astype(BF16)
    lo = (x - hi.astype(F32)).astype(BF16)
    return hi, lo


def _inproj_attn_kernel(x_ref, lnw_ref, w_ref, wt_ref, kw_ref, qwb_ref, o_ref, qt_ref, vt_ref, km_ref, h_scr, *, tm):
    j = pl.program_id(1)
    d = D_ATTN
    nblk = tm // BLOCK

    @pl.when(j == 0)
    def _():
        x = x_ref[...]
        ms = jnp.mean(x * x, axis=-1, keepdims=True)
        h_scr[...] = (x * lax.rsqrt(ms + EPS) * lnw_ref[...]).astype(BF16)
        for c in range(d // 256):
            acc = _dot(h_scr[...], w_ref[:, c * 256:(c + 1) * 256])
            for hh in range(2):
                lo = c * 256 + hh * HEAD_DIM
                a = acc[:, hh * HEAD_DIM:(hh + 1) * HEAD_DIM]
                y = a * lax.rsqrt(jnp.mean(a * a, axis=-1, keepdims=True) + EPS) * kw_ref[...]
                o_ref[:, lo:lo + HEAD_DIM] = y.astype(BF16)
                for b in range(nblk):
                    km_ref[0, b:b + 1, lo:lo + HEAD_DIM] = jnp.mean(
                        y[b * BLOCK:(b + 1) * BLOCK], axis=0, keepdims=True)

    @pl.when(j == 1)
    def _():
        for c in range(d // 256):
            acc = _dot(h_scr[...], w_ref[:, c * 256:(c + 1) * 256])
            o_ref[:, c * 256:(c + 1) * 256] = (acc * _sigmoid(acc)).astype(BF16)

    @pl.when(j == 2)
    def _():
        for b in range(nblk):
            a = _dot_nt(wt_ref[0], h_scr[b * BLOCK:(b + 1) * BLOCK, :])
            for hh in range(N_HEADS):
                hs = slice(hh * HEAD_DIM, (hh + 1) * HEAD_DIM)
                ah = a[hs, :]
                y = ah * lax.rsqrt(jnp.mean(ah * ah, axis=0, keepdims=True) + EPS) * qwb_ref[...]
                qt_ref[b, hs, :] = y.astype(BF16)

    @pl.when(j == 3)
    def _():
        for b in range(nblk):
            vt_ref[b] = _dot_nt(wt_ref[0], h_scr[b * BLOCK:(b + 1) * BLOCK, :]).astype(BF16)


def _inproj_attn(x2, ln_w, w_kg, w_t, kw, qwb, *, tm):
    m, d = x2.shape
    nblk = tm // BLOCK
    grid = (m // tm, 4)
    t_spec = pl.BlockSpec((nblk, D_ATTN, BLOCK), lambda i, j: (i, 0, 0))
    t_shape = jax.ShapeDtypeStruct((m // BLOCK, D_ATTN, BLOCK), BF16)
    return pl.pallas_call(
        functools.partial(_inproj_attn_kernel, tm=tm),
        grid=grid,
        in_specs=[
            pl.BlockSpec((tm, d), lambda i, j: (i, 0)),
            pl.BlockSpec((1, d), lambda i, j: (0, 0)),
            pl.BlockSpec((d, D_ATTN), lambda i, j: (0, jnp.minimum(j, 1))),
            pl.BlockSpec((1, D_ATTN, d), lambda i, j: (jnp.maximum(j - 2, 0), 0, 0)),
            pl.BlockSpec((1, HEAD_DIM), lambda i, j: (0, 0)),
            pl.BlockSpec((HEAD_DIM, BLOCK), lambda i, j: (0, 0)),
        ],
        out_specs=[
            pl.BlockSpec((tm, D_ATTN), lambda i, j: (i, jnp.minimum(j, 1))),
            t_spec,
            t_spec,
            pl.BlockSpec((1, nblk, D_ATTN), lambda i, j: (i, 0, 0)),
        ],
        out_shape=[
            jax.ShapeDtypeStruct((m, 2 * D_ATTN), BF16),
            t_shape,
            t_shape,
            jax.ShapeDtypeStruct((m // tm, nblk, D_ATTN), F32),
        ],
        scratch_shapes=[pltpu.VMEM((tm, d), BF16)],
        compiler_params=pltpu.CompilerParams(
            dimension_semantics=("arbitrary", "arbitrary"), vmem_limit_bytes=VMEM_LIMIT),
        name="inproj_attn",
    )(x2, ln_w, w_kg, w_t, kw, qwb)


def _attn_kernel(qt_ref, k_ref, vt_ref, g_ref, km_ref, acol_ref, adiag_ref, srow_ref, o_ref,
                 off_scr, mxa_scr, mxb_scr, m_scr, a_scr, l_scr, acc_scr, *sp_scrs, nb, hg):
    s_scrs, p_scrs = (sp_scrs[:hg], sp_scrs[hg:2 * hg]), sp_scrs[2 * hg:]
    mx_scrs = (mxa_scr, mxb_scr)
    i = pl.program_id(2)
    blk = lax.broadcasted_iota(jnp.int32, (nb, BLOCK), 0)
    blk_f = blk.astype(F32)
    elig = blk < i
    ones16 = jnp.ones((16, BLOCK), BF16)

    def put_scores(hh, par, s):
        s_scrs[par][hh][...] = s
        mx_scrs[par][hh] = jnp.max(s, axis=0, keepdims=True)

    def past_bias(hh):
        return jnp.concatenate([acol_ref[hh]] * (BLOCK // LANES), axis=1)

    def first_scores(hh):
        hs = slice(hh * HEAD_DIM, (hh + 1) * HEAD_DIM)
        put_scores(hh, 0, _dot(k_ref[i, :, hs], qt_ref[hs, :]) + adiag_ref[hh])
        put_scores(hh, 1, _dot(k_ref[0, :, hs], qt_ref[hs, :]) + past_bias(hh))

    def pair_scores(hh, k0):
        hs = slice(hh * HEAD_DIM, (hh + 1) * HEAD_DIM)
        r = _dot(k_ref[pl.ds(k0, 2), :, hs].reshape(2 * BLOCK, HEAD_DIM), qt_ref[hs, :])
        bias = past_bias(hh)
        put_scores(hh, 0, r[0:BLOCK] + bias)
        put_scores(hh, 1, r[BLOCK:2 * BLOCK] + bias)

    for hh in range(hg):
        hs = slice(hh * HEAD_DIM, (hh + 1) * HEAD_DIM)
        qt = qt_ref[hs, :]
        km_hi, km_lo = _split2(km_ref[:, hs])
        gate = _dot(km_hi, qt) + _dot(km_lo, qt)
        left = elig
        for _ in range(TOPK):
            g_left = jnp.where(left, gate, -jnp.inf)
            top = (g_left == jnp.max(g_left, axis=0, keepdims=True)) & left
            first = jnp.min(jnp.where(top, blk_f, float(nb)), axis=0, keepdims=True)
            left = left & (blk_f != first)
        sel = elig & jnp.logical_not(left)
        off = srow_ref[hh] * ((blk - i) * BLOCK).astype(F32) + jnp.where(sel, 0.0, NEG)
        off_scr[hh] = jnp.where(blk == 0, 0.0, pltpu.roll(off, 1, axis=0))
        first_scores(hh)
        m_scr[hh] = jnp.full((1, BLOCK), NEG, F32)
        a_scr[hh] = jnp.ones((1, BLOCK), F32)
        l_scr[hh] = jnp.zeros((8, BLOCK), F32)
        acc_scr[hh] = jnp.zeros((HEAD_DIM, BLOCK), F32)
        p_scrs[hh][...] = jnp.zeros((BLOCK, BLOCK), BF16)

    def pv_ones(hh, v_blk, p):
        hs = slice(hh * HEAD_DIM, (hh + 1) * HEAD_DIM)
        return _dot(jnp.concatenate([vt_ref[v_blk, hs, :], ones16], axis=0), p)

    def fold_prev(hh, v_blk):
        a_prev = a_scr[hh]
        pv = pv_ones(hh, v_blk, p_scrs[hh][...])
        return a_prev * acc_scr[hh] + pv[0:HEAD_DIM], a_prev * l_scr[hh] + pv[HEAD_DIM:HEAD_DIM + 8]

    def softmax_step(hh, j, par, m_old):
        off = off_scr[hh, pl.ds(j, 1), :]
        m_new = jnp.maximum(m_old, mx_scrs[par][hh] + off)
        p = jnp.exp2(s_scrs[par][hh][...] - (m_new - off)).astype(BF16)
        return p, jnp.exp2(m_old - m_new), m_new

    def block_of_step(j):
        return jnp.where(j == 0, i, jnp.maximum(j - 1, 0))

    def two_steps(hh, j0):
        acc, l = fold_prev(hh, block_of_step(jnp.maximum(j0 - 1, 0)))
        p0, a0, m0 = softmax_step(hh, j0, 0, m_scr[hh])
        p1, a1, m1 = softmax_step(hh, j0 + 1, 1, m0)
        pv = pv_ones(hh, block_of_step(j0), p0)
        return a0 * acc + pv[0:HEAD_DIM], a0 * l + pv[HEAD_DIM:HEAD_DIM + 8], p1, a1, m1

    def body(t, _):
        j0 = 2 * t
        for hh in range(hg):
            acc, l, p1, a1, m1 = two_steps(hh, j0)
            pair_scores(hh, j0 + 1)
            acc_scr[hh], l_scr[hh], p_scrs[hh][...], a_scr[hh], m_scr[hh] = acc, l, p1, a1, m1
        return 0

    half_i = lax.shift_right_logical(i, 1)
    lax.fori_loop(0, half_i, body, 0)
    j0 = 2 * half_i
    for hh in range(hg):
        hs = slice(hh * HEAD_DIM, (hh + 1) * HEAD_DIM)
        acc, l, p1, a1, _ = two_steps(hh, j0)
        pv = pv_ones(hh, block_of_step(j0 + 1), p1)
        acc = a1 * acc + pv[0:HEAD_DIM]
        l = a1 * l + pv[HEAD_DIM:HEAD_DIM + 8]
        out = (acc / l[0:1, :]).T
        o_ref[:, hs] = (out * g_ref[:, hs].astype(F32)).astype(BF16)


def _moba_attn(kg, qt, vt, kmean, alibi, *, batch, seq, hg):
    nb = seq // BLOCK
    assert nb % 2 == 0
    m = batch * seq
    hw = hg * HEAD_DIM
    ng = N_HEADS // hg
    kg3 = kg.reshape(m // BLOCK, BLOCK, 2 * D_ATTN)
    km2 = kmean.reshape(m // BLOCK, D_ATTN)
    return pl.pallas_call(
        functools.partial(_attn_kernel, nb=nb, hg=hg),
        grid=(batch, ng, nb),
        in_specs=[
            pl.BlockSpec((None, hw, BLOCK), lambda b, h, i: (b * nb + i, h, 0)),
            pl.BlockSpec((nb, BLOCK, hw), lambda b, h, i: (b, 0, h)),
            pl.BlockSpec((nb, hw, BLOCK), lambda b, h, i: (b, h, 0)),
            pl.BlockSpec((None, BLOCK, hw), lambda b, h, i: (b * nb + i, 0, ng + h)),
            pl.BlockSpec((nb, hw), lambda b, h, i: (b, h)),
            pl.BlockSpec((hg, BLOCK, LANES), lambda b, h, i: (h, 0, 0)),
            pl.BlockSpec((hg, BLOCK, BLOCK), lambda b, h, i: (h, 0, 0)),
            pl.BlockSpec((hg, 1, BLOCK), lambda b, h, i: (h, 0, 0)),
        ],
        out_specs=pl.BlockSpec((None, BLOCK, hw), lambda b, h, i: (b * nb + i, 0, h)),
        out_shape=jax.ShapeDtypeStruct((m // BLOCK, BLOCK, D_ATTN), BF16),
        scratch_shapes=[
            pltpu.VMEM((hg, nb, BLOCK), F32),
            pltpu.VMEM((hg, 1, BLOCK), F32),
            pltpu.VMEM((hg, 1, BLOCK), F32),
            pltpu.VMEM((hg, 1, BLOCK), F32),
            pltpu.VMEM((hg, 1, BLOCK), F32),
            pltpu.VMEM((hg, 8, BLOCK), F32),
            pltpu.VMEM((hg, HEAD_DIM, BLOCK), F32),
        ] + [pltpu.VMEM((BLOCK, BLOCK), F32) for _ in range(2 * hg)]
          + [pltpu.VMEM((BLOCK, BLOCK), BF16) for _ in range(hg)],
        compiler_params=pltpu.CompilerParams(
            dimension_semantics=("arbitrary", "arbitrary", "arbitrary"), vmem_limit_bytes=VMEM_LIMIT),
        name="moba_attn",
    )(qt, kg3, vt, kg3, km2, *alibi).reshape(m, D_ATTN)


def _alibi_tables():
    slopes = (LOG2E * np.exp2(-8.0 * np.arange(1, N_HEADS + 1, dtype=np.float64) / N_HEADS)).astype(np.float32)
    c = np.arange(BLOCK, dtype=np.float32)
    past = slopes[:, None, None] * np.broadcast_to(c[:, None], (BLOCK, BLOCK))[None]
    causal = np.where(c[:, None] <= c[None, :], 0.0, NEG).astype(np.float32)
    acol = past[:, :, :LANES].astype(np.float32)
    adiag = (past + causal[None]).astype(np.float32)
    srow = np.broadcast_to(slopes[:, None, None], (N_HEADS, 1, BLOCK)).astype(np.float32)
    return jnp.asarray(acol), jnp.asarray(adiag), jnp.asarray(srow)


def _ssd_chunk(p_ref, cw_ref, cb_ref, dtb_ref, alog_ref, dsk_ref, nw_ref, tri_ref, ex_ref,
               o_ref, xpad_scr, st_scr, y_scr, *, d_ssd):
    d_conv = d_ssd + 2 * SSD_GROUPS * SSD_STATE
    gw = d_ssd // SSD_GROUPS

    z = p_ref[:, 0:d_ssd]
    u = p_ref[:, d_ssd:d_ssd + d_conv]
    dtr = p_ref[:, d_ssd + d_conv:d_ssd + d_conv + LANES]

    xpad_scr[8:8 + BLOCK, :] = u
    conv = cb_ref[...] + cw_ref[CONV_W - 1:CONV_W, :] * u
    for s in range(1, CONV_W):
        conv = conv + cw_ref[CONV_W - 1 - s:CONV_W - s, :] * xpad_scr[8 - s:8 - s + BLOCK, :]
    xpad_scr[0:8, :] = u[BLOCK - 8:BLOCK, :]
    act = conv * _sigmoid(conv)
    xs = act[:, 0:d_ssd]
    bm = act[:, d_ssd:d_ssd + SSD_GROUPS * SSD_STATE]
    cm = act[:, d_ssd + SSD_GROUPS * SSD_STATE:d_conv]

    v = dtr + dtb_ref[...]
    dt = jnp.maximum(v, 0.0) + jnp.log1p(jnp.exp(-jnp.abs(v)))
    da = dt * (-jnp.exp(alog_ref[...]))
    hi = da.astype(BF16)
    r1 = da - hi.astype(F32)
    mid = r1.astype(BF16)
    lo = (r1 - mid.astype(F32)).astype(BF16)
    tri = tri_ref[...]
    acs = _dot(tri, hi) + _dot(tri, mid) + _dot(tri, lo)
    acs_t = acs.T
    e1 = jnp.exp(acs)
    e2 = jnp.exp(acs[BLOCK - 1:BLOCK, :] - acs)

    def expand(a):
        a_hi, a_lo = _split2(a)
        return _dot(jnp.concatenate([a_hi, a_lo], axis=1), ex_ref[...])

    dtx = expand(dt)
    e1x = expand(e1)
    w2x = expand(dt * e2)
    xdt_b = (xs * dtx).astype(BF16)

    row = lax.broadcasted_iota(jnp.int32, (BLOCK, BLOCK), 0)
    col = lax.broadcasted_iota(jnp.int32, (BLOCK, BLOCK), 1)
    causal = row >= col
    lane = lax.broadcasted_iota(jnp.int32, (BLOCK, LANES), 1)
    heads_per_group = gw // SSD_HEAD
    for g in range(SSD_GROUPS):
        bg = bm[:, g * SSD_STATE:(g + 1) * SSD_STATE]
        cg_b = cm[:, g * SSD_STATE:(g + 1) * SSD_STATE].astype(BF16)
        gmat = _dot_nt(cg_b, bg.astype(BF16))
        for pr in range(heads_per_group // 2):
            slab = slice(g * gw + pr * LANES, g * gw + (pr + 1) * LANES)
            xp = xdt_b[:, slab]
            pair = None
            for hh in range(2):
                h = g * heads_per_group + pr * 2 + hh
                diff = acs[:, h:h + 1] - acs_t[h:h + 1, :]
                w = (gmat * jnp.exp(jnp.where(causal, diff, -jnp.inf))).astype(BF16)
                keep = (lane < SSD_HEAD) if hh == 0 else (lane >= SSD_HEAD)
                part = _dot(w, jnp.where(keep, xp, jnp.zeros_like(xp)))
                pair = part if pair is None else pair + part
            y_scr[:, slab] = pair
        gsl = slice(g * gw, (g + 1) * gw)
        st = st_scr[g]
        y_scr[:, gsl] = y_scr[:, gsl] + _dot(cg_b, st.astype(BF16)) * e1x[:, gsl]
        xw = (xs[:, gsl] * w2x[:, gsl]).astype(BF16)
        st_scr[g] = st * e1x[BLOCK - 1:BLOCK, gsl] + _dot(bg.T.astype(BF16), xw)

    y = y_scr[...] + xs * dsk_ref[...]
    y = y * (z * _sigmoid(z))
    for g in range(SSD_GROUPS):
        gsl = slice(g * gw, (g + 1) * gw)
        yg = y[:, gsl]
        ms = jnp.mean(yg * yg, axis=-1, keepdims=True)
        o_ref[:, gsl] = (yg * lax.rsqrt(ms + EPS) * nw_ref[:, gsl]).astype(BF16)


def _ssd_out_kernel(x_ref, lnw_ref, w_ref, cw_ref, cb_ref, dtb_ref, alog_ref, dsk_ref, nw_ref, tri_ref, ex_ref,
                    attn_ref, wout_ref, o_ref, proj_scr, yb_scr, xpad_scr, st_scr, y_scr, *, d_ssd, cps):
    c = pl.program_id(1)
    d_conv = d_ssd + 2 * SSD_GROUPS * SSD_STATE

    @pl.when(c == 0)
    def _():
        xpad_scr[0:8, :] = jnp.zeros((8, d_conv), F32)
        st_scr[...] = jnp.zeros_like(st_scr)

    x = x_ref[...]
    ms = jnp.mean(x * x, axis=-1, keepdims=True)
    h = (x * lax.rsqrt(ms + EPS) * lnw_ref[...]).astype(BF16)
    n = w_ref.shape[1]
    step = 2 * MXU_COLS
    for lo in range(0, n, step):
        sl = slice(lo, min(lo + step, n))
        proj_scr[:, sl] = _dot(h, w_ref[:, sl])
    for ck in range(cps):
        rows = pl.ds(ck * BLOCK, BLOCK)
        _ssd_chunk(proj_scr.at[rows], cw_ref, cb_ref, dtb_ref, alog_ref, dsk_ref, nw_ref, tri_ref, ex_ref,
                   yb_scr.at[rows], xpad_scr, st_scr, y_scr, d_ssd=d_ssd)
    da = attn_ref.shape[1]
    o_ref[...] = x_ref[...] + _dot(attn_ref[...], wout_ref[0:da, :]) + _dot(yb_scr[...], wout_ref[da:, :])


def _ssd_out(x2, ln_w, w_ssd, cw, cb, dtb, alog, dsk, nw, tri, ex, attn, w_out, *, batch, seq, d_ssd):
    nc = seq // BLOCK
    cps = 2 if nc % 2 == 0 else 1
    rows = cps * BLOCK
    steps = nc // cps
    m, d = x2.shape
    n = w_ssd.shape[1]
    d_conv = d_ssd + 2 * SSD_GROUPS * SSD_STATE
    gw = d_ssd // SSD_GROUPS
    const = lambda b, c: (0, 0)
    tile = lambda b, c: (b * steps + c, 0)
    return pl.pallas_call(
        functools.partial(_ssd_out_kernel, d_ssd=d_ssd, cps=cps),
        grid=(batch, steps),
        in_specs=[
            pl.BlockSpec((rows, d), tile),
            pl.BlockSpec((1, d), const),
            pl.BlockSpec((d, n), const),
            pl.BlockSpec((CONV_W, d_conv), const),
            pl.BlockSpec((1, d_conv), const),
            pl.BlockSpec((1, LANES), const),
            pl.BlockSpec((1, LANES), const),
            pl.BlockSpec((1, d_ssd), const),
            pl.BlockSpec((1, d_ssd), const),
            pl.BlockSpec((BLOCK, BLOCK), const),
            pl.BlockSpec((2 * LANES, d_ssd), const),
            pl.BlockSpec((rows, attn.shape[1]), tile),
            pl.BlockSpec(w_out.shape, const),
        ],
        out_specs=pl.BlockSpec((rows, d), tile),
        out_shape=jax.ShapeDtypeStruct((m, d), F32),
        scratch_shapes=[
            pltpu.VMEM((rows, n), F32),
            pltpu.VMEM((rows, d_ssd), BF16),
            pltpu.VMEM((8 + BLOCK, d_conv), F32),
            pltpu.VMEM((SSD_GROUPS, SSD_STATE, gw), F32),
            pltpu.VMEM((BLOCK, d_ssd), F32),
        ],
        compiler_params=pltpu.CompilerParams(
            dimension_semantics=("arbitrary", "arbitrary"), vmem_limit_bytes=VMEM_LIMIT),
        name="ssd_out",
    )(x2, ln_w, w_ssd, cw, cb, dtb, alog, dsk, nw, tri, ex, attn, w_out)


def _row_tile(m, want):
    t = want
    while m % t:
        t //= 2
    return t


def _layer(x2, ln_w, w_in, q_norm_w, k_norm_w, conv_w, conv_b, dt_bias, a_log, d_skip, ssd_norm_w, w_out,
           tables, *, batch, seq):
    d = x2.shape[1]
    d_ssd = w_out.shape[0] - D_ATTN
    d_conv = d_ssd + 2 * SSD_GROUPS * SSD_STATE
    alibi, tri, ex = tables
    m = x2.shape[0]

    wq, wk, wv, wg = (w_in[:, k * D_ATTN:(k + 1) * D_ATTN] for k in range(4))
    w_kg = jnp.concatenate([wk, wg], axis=1).astype(BF16)
    w_t = jnp.stack([wq.T, wv.T]).astype(BF16)
    qwb = jnp.broadcast_to((q_norm_w * (HEAD_DIM ** -0.5 * LOG2E))[:, None], (HEAD_DIM, BLOCK))
    w_rest = w_in[:, 4 * D_ATTN:]
    pad = (-w_rest.shape[1]) % LANES
    w_ssd = jnp.pad(w_rest, ((0, 0), (0, pad))).astype(BF16)
    lnw = ln_w.reshape(1, d)

    def lane_pad(v):
        return jnp.pad(v, (0, LANES - v.shape[0])).reshape(1, LANES)

    kg, qt, vt, kmean = _inproj_attn(x2, lnw, w_kg, w_t, k_norm_w.reshape(1, HEAD_DIM), qwb,
                                     tm=_row_tile(m, 1024))
    attn = _moba_attn(kg, qt, vt, kmean, alibi, batch=batch, seq=seq, hg=ATTN_HEADS_PER_STEP)
    return _ssd_out(x2, lnw, w_ssd, conv_w, conv_b.reshape(1, d_conv), lane_pad(dt_bias), lane_pad(a_log),
                    jnp.repeat(d_skip, SSD_HEAD).reshape(1, d_ssd), ssd_norm_w.reshape(1, d_ssd), tri, ex,
                    attn, w_out.astype(BF16), batch=batch, seq=seq, d_ssd=d_ssd)


def kernel(x, ln_w, w_in, q_norm_w, k_norm_w, conv_w, conv_b, dt_bias, a_log, d_skip, ssd_norm_w, w_out):
    batch, seq, d = x.shape
    assert seq % BLOCK == 0 and d == D_ATTN
    d_ssd = w_out.shape[1] - D_ATTN
    n_heads = d_ssd // SSD_HEAD
    alibi = _alibi_tables()
    tri = jnp.asarray(np.tril(np.ones((BLOCK, BLOCK), np.float32))).astype(BF16)
    ex_np = np.zeros((LANES, d_ssd), np.float32)
    ex_np[np.repeat(np.arange(n_heads), SSD_HEAD), np.arange(d_ssd)] = 1.0
    ex = jnp.asarray(np.concatenate([ex_np, ex_np], axis=0)).astype(BF16)
    tables = (alibi, tri, ex)
    h = x.reshape(batch * seq, d)
    for i in range(ln_w.shape[0]):
        h = _layer(h, ln_w[i], w_in[i], q_norm_w[i], k_norm_w[i], conv_w[i], conv_b[i], dt_bias[i],
                   a_log[i], d_skip[i], ssd_norm_w[i], w_out[i], tables, batch=batch, seq=seq)
    return h.reshape(batch, seq, d)
```

```python
import functools

import jax
import jax.numpy as jnp
import numpy as np
from jax import lax
from jax.experimental import pallas as pl
from jax.experimental.pallas import tpu as pltpu

F32 = jnp.float32
BF16 = jnp.bfloat16

HEAD_DIM = 128
N_HEADS = 8
D_ATTN = N_HEADS * HEAD_DIM
BLOCK = 256
TOPK = 3
SSD_HEAD = 64
SSD_STATE = 128
SSD_GROUPS = 2
CONV_W = 4
EPS = 1e-6
NEG = -1e30
LOG2E = 1.4426950408889634
ATTN_HEADS_PER_STEP = 8
LANES = 128
MXU_COLS = 256
VMEM_LIMIT = 56 * 1024 * 1024


def _sigmoid(x):
    return 1.0 / (1.0 + jnp.exp(-x))


def _dot_nt(a, b):
    return lax.dot_general(a, b, (((1,), (1,)), ((), ())), preferred_element_type=F32)


def _dot(a, b):
    return jnp.dot(a, b, preferred_element_type=F32)


def _split2(x):
    hi = x.astype(BF16)
    lo = (x - hi.astype(F32)).astype(BF16)
    return hi, lo


def _inproj_attn_kernel(x_ref, lnw_ref, w_ref, wt_ref, kw_ref, qwb_ref, o_ref, qt_ref, vt_ref, km_ref, h_scr, *, tm):
    j = pl.program_id(1)
    d = D_ATTN
    nblk = tm // BLOCK

    @pl.when(j == 0)
    def _():
        x = x_ref[...]
        ms = jnp.mean(x * x, axis=-1, keepdims=True)
        h_scr[...] = (x * lax.rsqrt(ms + EPS) * lnw_ref[...]).astype(BF16)
        for c in range(d // 256):
            acc = _dot(h_scr[...], w_ref[:, c * 256:(c + 1) * 256])
            for hh in range(2):
                lo = c * 256 + hh * HEAD_DIM
                a = acc[:, hh * HEAD_DIM:(hh + 1) * HEAD_DIM]
                y = a * lax.rsqrt(jnp.mean(a * a, axis=-1, keepdims=True) + EPS) * kw_ref[...]
                o_ref[:, lo:lo + HEAD_DIM] = y.astype(BF16)
                for b in range(nblk):
                    km_ref[0, b:b + 1, lo:lo + HEAD_DIM] = jnp.mean(
                        y[b * BLOCK:(b + 1) * BLOCK], axis=0, keepdims=True)

    @pl.when(j == 1)
    def _():
        for c in range(d // 256):
            acc = _dot(h_scr[...], w_ref[:, c * 256:(c + 1) * 256])
            o_ref[:, c * 256:(c + 1) * 256] = (acc * _sigmoid(acc)).astype(BF16)

    @pl.when(j == 2)
    def _():
        for b in range(nblk):
            a = _dot_nt(wt_ref[0], h_scr[b * BLOCK:(b + 1) * BLOCK, :])
            for hh in range(N_HEADS):
                hs = slice(hh * HEAD_DIM, (hh + 1) * HEAD_DIM)
                ah = a[hs, :]
                y = ah * lax.rsqrt(jnp.mean(ah * ah, axis=0, keepdims=True) + EPS) * qwb_ref[...]
                qt_ref[b, hs, :] = y.astype(BF16)

    @pl.when(j == 3)
    def _():
        for b in range(nblk):
            vt_ref[b] = _dot_nt(wt_ref[0], h_scr[b * BLOCK:(b + 1) * BLOCK, :]).astype(BF16)


def _inproj_attn(x2, ln_w, w_kg, w_t, kw, qwb, *, tm):
    m, d = x2.shape
    nblk = tm // BLOCK
    grid = (m // tm, 4)
    t_spec = pl.BlockSpec((nblk, D_ATTN, BLOCK), lambda i, j: (i, 0, 0))
    t_shape = jax.ShapeDtypeStruct((m // BLOCK, D_ATTN, BLOCK), BF16)
    return pl.pallas_call(
        functools.partial(_inproj_attn_kernel, tm=tm),
        grid=grid,
        in_specs=[
            pl.BlockSpec((tm, d), lambda i, j: (i, 0)),
            pl.BlockSpec((1, d), lambda i, j: (0, 0)),
            pl.BlockSpec((d, D_ATTN), lambda i, j: (0, jnp.minimum(j, 1))),
            pl.BlockSpec((1, D_ATTN, d), lambda i, j: (jnp.maximum(j - 2, 0), 0, 0)),
            pl.BlockSpec((1, HEAD_DIM), lambda i, j: (0, 0)),
            pl.BlockSpec((HEAD_DIM, BLOCK), lambda i, j: (0, 0)),
        ],
        out_specs=[
            pl.BlockSpec((tm, D_ATTN), lambda i, j: (i, jnp.minimum(j, 1))),
            t_spec,
            t_spec,
            pl.BlockSpec((1, nblk, D_ATTN), lambda i, j: (i, 0, 0)),
        ],
        out_shape=[
            jax.ShapeDtypeStruct((m, 2 * D_ATTN), BF16),
            t_shape,
            t_shape,
            jax.ShapeDtypeStruct((m // tm, nblk, D_ATTN), F32),
        ],
        scratch_shapes=[pltpu.VMEM((tm, d), BF16)],
        compiler_params=pltpu.CompilerParams(
            dimension_semantics=("arbitrary", "arbitrary"), vmem_limit_bytes=VMEM_LIMIT),
        name="inproj_attn",
    )(x2, ln_w, w_kg, w_t, kw, qwb)


def _attn_kernel(qt_ref, k_ref, vt_ref, g_ref, km_ref, acol_ref, adiag_ref, srow_ref, o_ref,
                 off_scr, mxa_scr, mxb_scr, m_scr, a_scr, l_scr, acc_scr, *sp_scrs, nb, hg):
    s_scrs, p_scrs = (sp_scrs[:hg], sp_scrs[hg:2 * hg]), sp_scrs[2 * hg:]
    mx_scrs = (mxa_scr, mxb_scr)
    i = pl.program_id(2)
    blk = lax.broadcasted_iota(jnp.int32, (nb, BLOCK), 0)
    blk_f = blk.astype(F32)
    elig = blk < i
    ones16 = jnp.ones((16, BLOCK), BF16)

    def put_scores(hh, par, s):
        s_scrs[par][hh][...] = s
        mx_scrs[par][hh] = jnp.max(s, axis=0, keepdims=True)

    def past_bias(hh):
        return jnp.concatenate([acol_ref[hh]] * (BLOCK // LANES), axis=1)

    def first_scores(hh):
        hs = slice(hh * HEAD_DIM, (hh + 1) * HEAD_DIM)
        put_scores(hh, 0, _dot(k_ref[i, :, hs], qt_ref[hs, :]) + adiag_ref[hh])
        put_scores(hh, 1, _dot(k_ref[0, :, hs], qt_ref[hs, :]) + past_bias(hh))

    def pair_scores(hh, k0):
        hs = slice(hh * HEAD_DIM, (hh + 1) * HEAD_DIM)
        r = _dot(k_ref[pl.ds(k0, 2), :, hs].reshape(2 * BLOCK, HEAD_DIM), qt_ref[hs, :])
        bias = past_bias(hh)
        put_scores(hh, 0, r[0:BLOCK] + bias)
        put_scores(hh, 1, r[BLOCK:2 * BLOCK] + bias)

    for hh in range(hg):
        hs = slice(hh * HEAD_DIM, (hh + 1) * HEAD_DIM)
        qt = qt_ref[hs, :]
        km_hi, km_lo = _split2(km_ref[:, hs])
        gate = _dot(km_hi, qt) + _dot(km_lo, qt)
        left = elig
        for _ in range(TOPK):
            g_left = jnp.where(left, gate, -jnp.inf)
            top = (g_left == jnp.max(g_left, axis=0, keepdims=True)) & left
            first = jnp.min(jnp.where(top, blk_f, float(nb)), axis=0, keepdims=True)
            left = left & (blk_f != first)
        sel = elig & jnp.logical_not(left)
        off = srow_ref[hh] * ((blk - i) * BLOCK).astype(F32) + jnp.where(sel, 0.0, NEG)
        off_scr[hh] = jnp.where(blk == 0, 0.0, pltpu.roll(off, 1, axis=0))
        first_scores(hh)
        m_scr[hh] = jnp.full((1, BLOCK), NEG, F32)
        a_scr[hh] = jnp.ones((1, BLOCK), F32)
        l_scr[hh] = jnp.zeros((8, BLOCK), F32)
        acc_scr[hh] = jnp.zeros((HEAD_DIM, BLOCK), F32)
        p_scrs[hh][...] = jnp.zeros((BLOCK, BLOCK), BF16)

    def pv_ones(hh, v_blk, p):
        hs = slice(hh * HEAD_DIM, (hh + 1) * HEAD_DIM)
        return _dot(jnp.concatenate([vt_ref[v_blk, hs, :], ones16], axis=0), p)

    def fold_prev(hh, v_blk):
        a_prev = a_scr[hh]
        pv = pv_ones(hh, v_blk, p_scrs[hh][...])
        return a_prev * acc_scr[hh] + pv[0:HEAD_DIM], a_prev * l_scr[hh] + pv[HEAD_DIM:HEAD_DIM + 8]

    def softmax_step(hh, j, par, m_old):
        off = off_scr[hh, pl.ds(j, 1), :]
        m_new = jnp.maximum(m_old, mx_scrs[par][hh] + off)
        p = jnp.exp2(s_scrs[par][hh][...] - (m_new - off)).astype(BF16)
        return p, jnp.exp2(m_old - m_new), m_new

    def block_of_step(j):
        return jnp.where(j == 0, i, jnp.maximum(j - 1, 0))

    def two_steps(hh, j0):
        acc, l = fold_prev(hh, block_of_step(jnp.maximum(j0 - 1, 0)))
        p0, a0, m0 = softmax_step(hh, j0, 0, m_scr[hh])
        p1, a1, m1 = softmax_step(hh, j0 + 1, 1, m0)
        pv = pv_ones(hh, block_of_step(j0), p0)
        return a0 * acc + pv[0:HEAD_DIM], a0 * l + pv[HEAD_DIM:HEAD_DIM + 8], p1, a1, m1

    def body(t, _):
        j0 = 2 * t
        for hh in range(hg):
            acc, l, p1, a1, m1 = two_steps(hh, j0)
            pair_scores(hh, j0 + 1)
            acc_scr[hh], l_scr[hh], p_scrs[hh][...], a_scr[hh], m_scr[hh] = acc, l, p1, a1, m1
        return 0

    half_i = lax.shift_right_logical(i, 1)
    lax.fori_loop(0, half_i, body, 0)
    j0 = 2 * half_i
    for hh in range(hg):
        hs = slice(hh * HEAD_DIM, (hh + 1) * HEAD_DIM)
        acc, l, p1, a1, _ = two_steps(hh, j0)
        pv = pv_ones(hh, block_of_step(j0 + 1), p1)
        acc = a1 * acc + pv[0:HEAD_DIM]
        l = a1 * l + pv[HEAD_DIM:HEAD_DIM + 8]
        out = (acc / l[0:1, :]).T
        o_ref[:, hs] = (out * g_ref[:, hs].astype(F32)).astype(BF16)


def _moba_attn(kg, qt, vt, kmean, alibi, *, batch, seq, hg):
    nb = seq // BLOCK
    assert nb % 2 == 0
    m = batch * seq
    hw = hg * HEAD_DIM
    ng = N_HEADS // hg
    kg3 = kg.reshape(m // BLOCK, BLOCK, 2 * D_ATTN)
    km2 = kmean.reshape(m // BLOCK, D_ATTN)
    return pl.pallas_call(
        functools.partial(_attn_kernel, nb=nb, hg=hg),
        grid=(batch, ng, nb),
        in_specs=[
            pl.BlockSpec((None, hw, BLOCK), lambda b, h, i: (b * nb + i, h, 0)),
            pl.BlockSpec((nb, BLOCK, hw), lambda b, h, i: (b, 0, h)),
            pl.BlockSpec((nb, hw, BLOCK), lambda b, h, i: (b, h, 0)),
            pl.BlockSpec((None, BLOCK, hw), lambda b, h, i: (b * nb + i, 0, ng + h)),
            pl.BlockSpec((nb, hw), lambda b, h, i: (b, h)),
            pl.BlockSpec((hg, BLOCK, LANES), lambda b, h, i: (h, 0, 0)),
            pl.BlockSpec((hg, BLOCK, BLOCK), lambda b, h, i: (h, 0, 0)),
            pl.BlockSpec((hg, 1, BLOCK), lambda b, h, i: (h, 0, 0)),
        ],
        out_specs=pl.BlockSpec((None, BLOCK, hw), lambda b, h, i: (b * nb + i, 0, h)),
        out_shape=jax.ShapeDtypeStruct((m // BLOCK, BLOCK, D_ATTN), BF16),
        scratch_shapes=[
            pltpu.VMEM((hg, nb, BLOCK), F32),
            pltpu.VMEM((hg, 1, BLOCK), F32),
            pltpu.VMEM((hg, 1, BLOCK), F32),
            pltpu.VMEM((hg, 1, BLOCK), F32),
            pltpu.VMEM((hg, 1, BLOCK), F32),
            pltpu.VMEM((hg, 8, BLOCK), F32),
            pltpu.VMEM((hg, HEAD_DIM, BLOCK), F32),
        ] + [pltpu.VMEM((BLOCK, BLOCK), F32) for _ in range(2 * hg)]
          + [pltpu.VMEM((BLOCK, BLOCK), BF16) for _ in range(hg)],
        compiler_params=pltpu.CompilerParams(
            dimension_semantics=("arbitrary", "arbitrary", "arbitrary"), vmem_limit_bytes=VMEM_LIMIT),
        name="moba_attn",
    )(qt, kg3, vt, kg3, km2, *alibi).reshape(m, D_ATTN)


def _alibi_tables():
    slopes = (LOG2E * np.exp2(-8.0 * np.arange(1, N_HEADS + 1, dtype=np.float64) / N_HEADS)).astype(np.float32)
    c = np.arange(BLOCK, dtype=np.float32)
    past = slopes[:, None, None] * np.broadcast_to(c[:, None], (BLOCK, BLOCK))[None]
    causal = np.where(c[:, None] <= c[None, :], 0.0, NEG).astype(np.float32)
    acol = past[:, :, :LANES].astype(np.float32)
    adiag = (past + causal[None]).astype(np.float32)
    srow = np.broadcast_to(slopes[:, None, None], (N_HEADS, 1, BLOCK)).astype(np.float32)
    return jnp.asarray(acol), jnp.asarray(adiag), jnp.asarray(srow)


def _ssd_chunk(p_ref, cw_ref, cb_ref, dtb_ref, alog_ref, dsk_ref, nw_ref, tri_ref, ex_ref,
               o_ref, xpad_scr, st_scr, y_scr, *, d_ssd, fillers=()):
    fillers = list(fillers)
    d_conv = d_ssd + 2 * SSD_GROUPS * SSD_STATE
    gw = d_ssd // SSD_GROUPS

    u = p_ref[:, d_ssd:d_ssd + d_conv]
    dtr = p_ref[:, d_ssd + d_conv:d_ssd + d_conv + LANES]

    xpad_scr[8:8 + BLOCK, :] = u
    conv = cb_ref[...] + cw_ref[CONV_W - 1:CONV_W, :] * u
    for s in range(1, CONV_W):
        conv = conv + cw_ref[CONV_W - 1 - s:CONV_W - s, :] * xpad_scr[8 - s:8 - s + BLOCK, :]
    xpad_scr[0:8, :] = u[BLOCK - 8:BLOCK, :]
    act = conv * _sigmoid(conv)
    xs = act[:, 0:d_ssd]
    bm = act[:, d_ssd:d_ssd + SSD_GROUPS * SSD_STATE]
    cm = act[:, d_ssd + SSD_GROUPS * SSD_STATE:d_conv]

    v = dtr + dtb_ref[...]
    dt = jnp.maximum(v, 0.0) + jnp.log1p(jnp.exp(-jnp.abs(v)))
    da = dt * (-jnp.exp(alog_ref[...]))
    hi = da.astype(BF16)
    r1 = da - hi.astype(F32)
    mid = r1.astype(BF16)
    lo = (r1 - mid.astype(F32)).astype(BF16)
    tri = tri_ref[...]
    acs = _dot(tri, hi) + _dot(tri, mid) + _dot(tri, lo)
    acs_t = acs.T
    e1 = jnp.exp(acs)
    e2 = jnp.exp(acs[BLOCK - 1:BLOCK, :] - acs)

    def expand(a):
        a_hi, a_lo = _split2(a)
        return _dot(jnp.concatenate([a_hi, a_lo], axis=1), ex_ref[...])

    dtx = expand(dt)
    e1x = expand(e1)
    w2x = expand(dt * e2)
    xdt_b = (xs * dtx).astype(BF16)

    row = lax.broadcasted_iota(jnp.int32, (BLOCK, BLOCK), 0)
    col = lax.broadcasted_iota(jnp.int32, (BLOCK, BLOCK), 1)
    causal = row >= col
    lane = lax.broadcasted_iota(jnp.int32, (BLOCK, LANES), 1)
    heads_per_group = gw // SSD_HEAD
    for g in range(SSD_GROUPS):
        bg = bm[:, g * SSD_STATE:(g + 1) * SSD_STATE]
        cg_b = cm[:, g * SSD_STATE:(g + 1) * SSD_STATE].astype(BF16)
        gmat = _dot_nt(cg_b, bg.astype(BF16))
        for pr in range(heads_per_group // 2):
            slab = slice(g * gw + pr * LANES, g * gw + (pr + 1) * LANES)
            xp = xdt_b[:, slab]
            pair = None
            for hh in range(2):
                h = g * heads_per_group + pr * 2 + hh
                diff = acs[:, h:h + 1] - acs_t[h:h + 1, :]
                w = (gmat * jnp.exp(jnp.where(causal, diff, -jnp.inf))).astype(BF16)
                keep = (lane < SSD_HEAD) if hh == 0 else (lane >= SSD_HEAD)
                part = _dot(w, jnp.where(keep, xp, jnp.zeros_like(xp)))
                pair = part if pair is None else pair + part
            y_scr[:, slab] = pair
            if fillers:
                fillers.pop(0)()
        gsl = slice(g * gw, (g + 1) * gw)
        st = st_scr[g]
        y_scr[:, gsl] = y_scr[:, gsl] + _dot(cg_b, st.astype(BF16)) * e1x[:, gsl]
        xw = (xs[:, gsl] * w2x[:, gsl]).astype(BF16)
        st_scr[g] = st * e1x[BLOCK - 1:BLOCK, gsl] + _dot(bg.T.astype(BF16), xw)

    y = y_scr[...] + xs * dsk_ref[...]
    z = p_ref[:, 0:d_ssd]
    y = y * (z * _sigmoid(z))
    for g in range(SSD_GROUPS):
        gsl = slice(g * gw, (g + 1) * gw)
        yg = y[:, gsl]
        ms = jnp.mean(yg * yg, axis=-1, keepdims=True)
        o_ref[:, gsl] = (yg * lax.rsqrt(ms + EPS) * nw_ref[:, gsl]).astype(BF16)
    for f in fillers:
        f()


def _ssd_out_kernel(x_ref, lnw_ref, w_ref, cw_ref, cb_ref, dtb_ref, alog_ref, dsk_ref, nw_ref, tri_ref, ex_ref,
                    attn_ref, wout_ref, o_ref, proj_scr, yb_scr, xpad_scr, st_scr, y_scr, *, d_ssd, cps):
    c = pl.program_id(1)
    d_conv = d_ssd + 2 * SSD_GROUPS * SSD_STATE

    @pl.when(c == 0)
    def _():
        xpad_scr[0:8, :] = jnp.zeros((8, d_conv), F32)
        st_scr[...] = jnp.zeros_like(st_scr)

    x = x_ref[...]
    ms = jnp.mean(x * x, axis=-1, keepdims=True)
    h = (x * lax.rsqrt(ms + EPS) * lnw_ref[...]).astype(BF16)
    n = w_ref.shape[1]
    step = 2 * MXU_COLS
    for lo in range(d_ssd, n, step):
        sl = slice(lo, min(lo + step, n))
        proj_scr[:, sl] = _dot(h, w_ref[:, sl])
    da = attn_ref.shape[1]
    d = o_ref.shape[1]

    def z_piece(sl):
        proj_scr[:, sl] = _dot(h, w_ref[:, sl])

    def attn_out_piece(sl):
        o_ref[:, sl] = x_ref[:, sl] + _dot(attn_ref[...], wout_ref[0:da, sl])

    fillers = [functools.partial(z_piece, slice(lo, lo + MXU_COLS)) for lo in range(0, d_ssd, MXU_COLS)]
    fillers += [functools.partial(attn_out_piece, slice(lo, lo + MXU_COLS)) for lo in range(0, d, MXU_COLS)]
    n_first = len(fillers) if cps == 1 else d_ssd // MXU_COLS + 2
    assert d_ssd // MXU_COLS <= d_ssd // (2 * SSD_HEAD)
    for ck in range(cps):
        rows = pl.ds(ck * BLOCK, BLOCK)
        mine = fillers[:n_first] if ck == 0 else (fillers[n_first:] if ck == cps - 1 else [])
        _ssd_chunk(proj_scr.at[rows], cw_ref, cb_ref, dtb_ref, alog_ref, dsk_ref, nw_ref, tri_ref, ex_ref,
                   yb_scr.at[rows], xpad_scr, st_scr, y_scr, d_ssd=d_ssd, fillers=mine)
    o_ref[...] += _dot(yb_scr[...], wout_ref[da:, :])


def _ssd_out(x2, ln_w, w_ssd, cw, cb, dtb, alog, dsk, nw, tri, ex, attn, w_out, *, batch, seq, d_ssd):
    nc = seq // BLOCK
    cps = 2 if nc % 2 == 0 else 1
    rows = cps * BLOCK
    steps = nc // cps
    m, d = x2.shape
    n = w_ssd.shape[1]
    d_conv = d_ssd + 2 * SSD_GROUPS * SSD_STATE
    gw = d_ssd // SSD_GROUPS
    const = lambda b, c: (0, 0)
    tile = lambda b, c: (b * steps + c, 0)
    return pl.pallas_call(
        functools.partial(_ssd_out_kernel, d_ssd=d_ssd, cps=cps),
        grid=(batch, steps),
        in_specs=[
            pl.BlockSpec((rows, d), tile),
            pl.BlockSpec((1, d), const),
            pl.BlockSpec((d, n), const),
            pl.BlockSpec((CONV_W, d_conv), const),
            pl.BlockSpec((1, d_conv), const),
            pl.BlockSpec((1, LANES), const),
            pl.BlockSpec((1, LANES), const),
            pl.BlockSpec((1, d_ssd), const),
            pl.BlockSpec((1, d_ssd), const),
            pl.BlockSpec((BLOCK, BLOCK), const),
            pl.BlockSpec((2 * LANES, d_ssd), const),
            pl.BlockSpec((rows, attn.shape[1]), tile),
            pl.BlockSpec(w_out.shape, const),
        ],
        out_specs=pl.BlockSpec((rows, d), tile),
        out_shape=jax.ShapeDtypeStruct((m, d), F32),
        scratch_shapes=[
            pltpu.VMEM((rows, n), F32),
            pltpu.VMEM((rows, d_ssd), BF16),
            pltpu.VMEM((8 + BLOCK, d_conv), F32),
            pltpu.VMEM((SSD_GROUPS, SSD_STATE, gw), F32),
            pltpu.VMEM((BLOCK, d_ssd), F32),
        ],
        compiler_params=pltpu.CompilerParams(
            dimension_semantics=("arbitrary", "arbitrary"), vmem_limit_bytes=VMEM_LIMIT),
        name="ssd_out",
    )(x2, ln_w, w_ssd, cw, cb, dtb, alog, dsk, nw, tri, ex, attn, w_out)


def _row_tile(m, want):
    t = want
    while m % t:
        t //= 2
    return t


def _layer(x2, ln_w, w_in, q_norm_w, k_norm_w, conv_w, conv_b, dt_bias, a_log, d_skip, ssd_norm_w, w_out,
           tables, *, batch, seq):
    d = x2.shape[1]
    d_ssd = w_out.shape[0] - D_ATTN
    d_conv = d_ssd + 2 * SSD_GROUPS * SSD_STATE
    alibi, tri, ex = tables
    m = x2.shape[0]

    wq, wk, wv, wg = (w_in[:, k * D_ATTN:(k + 1) * D_ATTN] for k in range(4))
    w_kg = jnp.concatenate([wk, wg], axis=1).astype(BF16)
    w_t = jnp.stack([wq.T, wv.T]).astype(BF16)
    qwb = jnp.broadcast_to((q_norm_w * (HEAD_DIM ** -0.5 * LOG2E))[:, None], (HEAD_DIM, BLOCK))
    w_rest = w_in[:, 4 * D_ATTN:]
    pad = (-w_rest.shape[1]) % LANES
    w_ssd = jnp.pad(w_rest, ((0, 0), (0, pad))).astype(BF16)
    lnw = ln_w.reshape(1, d)

    def lane_pad(v):
        return jnp.pad(v, (0, LANES - v.shape[0])).reshape(1, LANES)

    kg, qt, vt, kmean = _inproj_attn(x2, lnw, w_kg, w_t, k_norm_w.reshape(1, HEAD_DIM), qwb,
                                     tm=_row_tile(m, 1024))
    attn = _moba_attn(kg, qt, vt, kmean, alibi, batch=batch, seq=seq, hg=ATTN_HEADS_PER_STEP)
    return _ssd_out(x2, lnw, w_ssd, conv_w, conv_b.reshape(1, d_conv), lane_pad(dt_bias), lane_pad(a_log),
                    jnp.repeat(d_skip, SSD_HEAD).reshape(1, d_ssd), ssd_norm_w.reshape(1, d_ssd), tri, ex,
                    attn, w_out.astype(BF16), batch=batch, seq=seq, d_ssd=d_ssd)


def kernel(x, ln_w, w_in, q_norm_w, k_norm_w, conv_w, conv_b, dt_bias, a_log, d_skip, ssd_norm_w, w_out):
    batch, seq, d = x.shape
    assert seq % BLOCK == 0 and d == D_ATTN
    d_ssd = w_out.shape[1] - D_ATTN
    n_heads = d_ssd // SSD_HEAD
    alibi = _alibi_tables()
    tri = jnp.asarray(np.tril(np.ones((BLOCK, BLOCK), np.float32))).astype(BF16)
    ex_np = np.zeros((LANES, d_ssd), np.float32)
    ex_np[np.repeat(np.arange(n_heads), SSD_HEAD), np.arange(d_ssd)] = 1.0
    ex = jnp.asarray(np.concatenate([ex_np, ex_np], axis=0)).astype(BF16)
    tables = (alibi, tri, ex)
    h = x.reshape(batch * seq, d)
    for i in range(ln_w.shape[0]):
        h = _layer(h, ln_w[i], w_in[i], q_norm_w[i], k_norm_w[i], conv_w[i], conv_b[i], dt_bias[i],
                   a_log[i], d_skip[i], ssd_norm_w[i], w_out[i], tables, batch=batch, seq=seq)
    return h.reshape(batch, seq, d)
```

```python
import functools

import jax
import jax.numpy as jnp
import numpy as np
from jax import lax
from jax.experimental import pallas as pl
from jax.experimental.pallas import tpu as pltpu

F32 = jnp.float32
BF16 = jnp.bfloat16

HEAD_DIM = 128
N_HEADS = 8
D_ATTN = N_HEADS * HEAD_DIM
BLOCK = 256
TOPK = 3
SSD_HEAD = 64
SSD_STATE = 128
SSD_GROUPS = 2
CONV_W = 4
EPS = 1e-6
NEG = -1e30
LOG2E = 1.4426950408889634
ATTN_HEADS_PER_STEP = 8
LANES = 128
MXU_COLS = 256
VMEM_LIMIT = 56 * 1024 * 1024


def _sigmoid(x):
    return 1.0 / (1.0 + jnp.exp2(x * (-LOG2E)))


def _dot_nt(a, b):
    return lax.dot_general(a, b, (((1,), (1,)), ((), ())), preferred_element_type=F32)


def _dot(a, b):
    return jnp.dot(a, b, preferred_element_type=F32)


def _split2(x):
    hi = x.astype(BF16)
    lo = (x - hi.astype(F32)).astype(BF16)
    return hi, lo


def _inproj_attn_kernel(x_ref, lnw_ref, w_ref, wt_ref, kw_ref, qwb_ref, o_ref, qt_ref, vt_ref, km_ref, h_scr, *, tm):
    d = D_ATTN
    nblk = tm // BLOCK
    x = x_ref[...]
    ms = jnp.mean(x * x, axis=-1, keepdims=True)
    h_scr[...] = (x * lax.rsqrt(ms + EPS) * lnw_ref[...]).astype(BF16)

    wc = MXU_COLS
    for c in range(d // wc):
        acc = _dot(h_scr[...], w_ref[:, c * wc:(c + 1) * wc])
        for hh in range(wc // HEAD_DIM):
            lo = c * wc + hh * HEAD_DIM
            a = acc[:, hh * HEAD_DIM:(hh + 1) * HEAD_DIM]
            y = a * lax.rsqrt(jnp.mean(a * a, axis=-1, keepdims=True) + EPS) * kw_ref[...]
            o_ref[:, lo:lo + HEAD_DIM] = y.astype(BF16)
            for b in range(nblk):
                km_ref[0, b:b + 1, lo:lo + HEAD_DIM] = jnp.mean(
                    y[b * BLOCK:(b + 1) * BLOCK], axis=0, keepdims=True)

    for c in range(d // wc):
        acc = _dot(h_scr[...], w_ref[:, d + c * wc:d + (c + 1) * wc])
        o_ref[:, d + c * wc:d + (c + 1) * wc] = (acc * _sigmoid(acc)).astype(BF16)

    for b in range(nblk):
        hb = h_scr[b * BLOCK:(b + 1) * BLOCK, :]
        a = _dot_nt(wt_ref[0], hb)
        for hh in range(N_HEADS):
            hs = slice(hh * HEAD_DIM, (hh + 1) * HEAD_DIM)
            ah = a[hs, :]
            y = ah * lax.rsqrt(jnp.mean(ah * ah, axis=0, keepdims=True) + EPS) * qwb_ref[...]
            qt_ref[b, hs, :] = y.astype(BF16)
        vt_ref[b] = _dot_nt(wt_ref[1], hb).astype(BF16)


def _inproj_attn(x2, ln_w, w_kg, w_t, kw, qwb, *, tm):
    m, d = x2.shape
    nblk = tm // BLOCK
    const2 = lambda i: (0, 0)
    t_spec = pl.BlockSpec((nblk, D_ATTN, BLOCK), lambda i: (i, 0, 0))
    t_shape = jax.ShapeDtypeStruct((m // BLOCK, D_ATTN, BLOCK), BF16)
    return pl.pallas_call(
        functools.partial(_inproj_attn_kernel, tm=tm),
        grid=(m // tm,),
        in_specs=[
            pl.BlockSpec((tm, d), lambda i: (i, 0)),
            pl.BlockSpec((1, d), const2),
            pl.BlockSpec((d, 2 * D_ATTN), const2),
            pl.BlockSpec((2, D_ATTN, d), lambda i: (0, 0, 0)),
            pl.BlockSpec((1, HEAD_DIM), const2),
            pl.BlockSpec((HEAD_DIM, BLOCK), const2),
        ],
        out_specs=[
            pl.BlockSpec((tm, 2 * D_ATTN), lambda i: (i, 0)),
            t_spec,
            t_spec,
            pl.BlockSpec((1, nblk, D_ATTN), lambda i: (i, 0, 0)),
        ],
        out_shape=[
            jax.ShapeDtypeStruct((m, 2 * D_ATTN), BF16),
            t_shape,
            t_shape,
            jax.ShapeDtypeStruct((m // tm, nblk, D_ATTN), F32),
        ],
        scratch_shapes=[pltpu.VMEM((tm, d), BF16)],
        compiler_params=pltpu.CompilerParams(
            dimension_semantics=("arbitrary",), vmem_limit_bytes=VMEM_LIMIT),
        name="inproj_attn",
    )(x2, ln_w, w_kg, w_t, kw, qwb)


def _attn_kernel(qt_ref, k_ref, vt_ref, g_ref, km_ref, acol_ref, adiag_ref, srow_ref, o_ref,
                 off_scr, mxa_scr, mxb_scr, m_scr, a_scr, l_scr, acc_scr, *sp_scrs, nb, hg):
    s_scrs, p_scrs = (sp_scrs[:hg], sp_scrs[hg:2 * hg]), sp_scrs[2 * hg:]
    mx_scrs = (mxa_scr, mxb_scr)
    i = pl.program_id(2)
    blk = lax.broadcasted_iota(jnp.int32, (nb, BLOCK), 0)
    blk_f = blk.astype(F32)
    elig = blk < i
    ones16 = jnp.ones((16, BLOCK), BF16)

    def put_scores(hh, par, s):
        s_scrs[par][hh][...] = s
        mx_scrs[par][hh] = jnp.max(s, axis=0, keepdims=True)

    def past_bias(hh):
        return jnp.concatenate([acol_ref[hh]] * (BLOCK // LANES), axis=1)

    def first_scores(hh):
        hs = slice(hh * HEAD_DIM, (hh + 1) * HEAD_DIM)
        put_scores(hh, 0, _dot(k_ref[i, :, hs], qt_ref[hs, :]) + adiag_ref[hh])
        put_scores(hh, 1, _dot(k_ref[0, :, hs], qt_ref[hs, :]) + past_bias(hh))

    def pair_scores(hh, k0):
        hs = slice(hh * HEAD_DIM, (hh + 1) * HEAD_DIM)
        r = _dot(k_ref[pl.ds(k0, 2), :, hs].reshape(2 * BLOCK, HEAD_DIM), qt_ref[hs, :])
        bias = past_bias(hh)
        put_scores(hh, 0, r[0:BLOCK] + bias)
        put_scores(hh, 1, r[BLOCK:2 * BLOCK] + bias)

    for hh in range(hg):
        hs = slice(hh * HEAD_DIM, (hh + 1) * HEAD_DIM)
        qt = qt_ref[hs, :]
        km_hi, km_lo = _split2(km_ref[:, hs])
        gate = _dot(km_hi, qt) + _dot(km_lo, qt)
        left = elig
        for _ in range(TOPK):
            g_left = jnp.where(left, gate, -jnp.inf)
            top = (g_left == jnp.max(g_left, axis=0, keepdims=True)) & left
            first = jnp.min(jnp.where(top, blk_f, float(nb)), axis=0, keepdims=True)
            left = left & (blk_f != first)
        sel = elig & jnp.logical_not(left)
        off = srow_ref[hh] * ((blk - i) * BLOCK).astype(F32) + jnp.where(sel, 0.0, NEG)
        off_scr[hh] = jnp.where(blk == 0, 0.0, pltpu.roll(off, 1, axis=0))
        first_scores(hh)
        m_scr[hh] = jnp.full((1, BLOCK), NEG, F32)
        a_scr[hh] = jnp.ones((1, BLOCK), F32)
        l_scr[hh] = jnp.zeros((8, BLOCK), F32)
        acc_scr[hh] = jnp.zeros((HEAD_DIM, BLOCK), F32)
        p_scrs[hh][...] = jnp.zeros((BLOCK, BLOCK), BF16)

    def pv_ones(hh, v_blk, p):
        hs = slice(hh * HEAD_DIM, (hh + 1) * HEAD_DIM)
        return _dot(jnp.concatenate([vt_ref[v_blk, hs, :], ones16], axis=0), p)

    def fold_prev(hh, v_blk):
        a_prev = a_scr[hh]
        pv = pv_ones(hh, v_blk, p_scrs[hh][...])
        return a_prev * acc_scr[hh] + pv[0:HEAD_DIM], a_prev * l_scr[hh] + pv[HEAD_DIM:HEAD_DIM + 8]

    def softmax_step(hh, j, par, m_old):
        off = off_scr[hh, pl.ds(j, 1), :]
        m_new = jnp.maximum(m_old, mx_scrs[par][hh] + off)
        p = jnp.exp2(s_scrs[par][hh][...] - (m_new - off)).astype(BF16)
        return p, jnp.exp2(m_old - m_new), m_new

    def block_of_step(j):
        return jnp.where(j == 0, i, jnp.maximum(j - 1, 0))

    def two_steps(hh, j0):
        acc, l = fold_prev(hh, block_of_step(jnp.maximum(j0 - 1, 0)))
        p0, a0, m0 = softmax_step(hh, j0, 0, m_scr[hh])
        p1, a1, m1 = softmax_step(hh, j0 + 1, 1, m0)
        pv = pv_ones(hh, block_of_step(j0), p0)
        return a0 * acc + pv[0:HEAD_DIM], a0 * l + pv[HEAD_DIM:HEAD_DIM + 8], p1, a1, m1

    def body(t, _):
        j0 = 2 * t
        for hh in range(hg):
            acc, l, p1, a1, m1 = two_steps(hh, j0)
            pair_scores(hh, j0 + 1)
            acc_scr[hh], l_scr[hh], p_scrs[hh][...], a_scr[hh], m_scr[hh] = acc, l, p1, a1, m1
        return 0

    half_i = lax.shift_right_logical(i, 1)
    lax.fori_loop(0, half_i, body, 0)
    j0 = 2 * half_i
    for hh in range(hg):
        hs = slice(hh * HEAD_DIM, (hh + 1) * HEAD_DIM)
        acc, l, p1, a1, _ = two_steps(hh, j0)
        pv = pv_ones(hh, block_of_step(j0 + 1), p1)
        acc = a1 * acc + pv[0:HEAD_DIM]
        l = a1 * l + pv[HEAD_DIM:HEAD_DIM + 8]
        out = (acc / l[0:1, :]).T
        o_ref[:, hs] = (out * g_ref[:, hs].astype(F32)).astype(BF16)


def _moba_attn(kg, qt, vt, kmean, alibi, *, batch, seq, hg):
    nb = seq // BLOCK
    assert nb % 2 == 0
    m = batch * seq
    hw = hg * HEAD_DIM
    ng = N_HEADS // hg
    kg3 = kg.reshape(m // BLOCK, BLOCK, 2 * D_ATTN)
    km2 = kmean.reshape(m // BLOCK, D_ATTN)
    return pl.pallas_call(
        functools.partial(_attn_kernel, nb=nb, hg=hg),
        grid=(batch, ng, nb),
        in_specs=[
            pl.BlockSpec((None, hw, BLOCK), lambda b, h, i: (b * nb + i, h, 0)),
            pl.BlockSpec((nb, BLOCK, hw), lambda b, h, i: (b, 0, h)),
            pl.BlockSpec((nb, hw, BLOCK), lambda b, h, i: (b, h, 0)),
            pl.BlockSpec((None, BLOCK, hw), lambda b, h, i: (b * nb + i, 0, ng + h)),
            pl.BlockSpec((nb, hw), lambda b, h, i: (b, h)),
            pl.BlockSpec((hg, BLOCK, LANES), lambda b, h, i: (h, 0, 0)),
            pl.BlockSpec((hg, BLOCK, BLOCK), lambda b, h, i: (h, 0, 0)),
            pl.BlockSpec((hg, 1, BLOCK), lambda b, h, i: (h, 0, 0)),
        ],
        out_specs=pl.BlockSpec((None, BLOCK, hw), lambda b, h, i: (b * nb + i, 0, h)),
        out_shape=jax.ShapeDtypeStruct((m // BLOCK, BLOCK, D_ATTN), BF16),
        scratch_shapes=[
            pltpu.VMEM((hg, nb, BLOCK), F32),
            pltpu.VMEM((hg, 1, BLOCK), F32),
            pltpu.VMEM((hg, 1, BLOCK), F32),
            pltpu.VMEM((hg, 1, BLOCK), F32),
            pltpu.VMEM((hg, 1, BLOCK), F32),
            pltpu.VMEM((hg, 8, BLOCK), F32),
            pltpu.VMEM((hg, HEAD_DIM, BLOCK), F32),
        ] + [pltpu.VMEM((BLOCK, BLOCK), F32) for _ in range(2 * hg)]
          + [pltpu.VMEM((BLOCK, BLOCK), BF16) for _ in range(hg)],
        compiler_params=pltpu.CompilerParams(
            dimension_semantics=("arbitrary", "arbitrary", "arbitrary"), vmem_limit_bytes=VMEM_LIMIT),
        name="moba_attn",
    )(qt, kg3, vt, kg3, km2, *alibi).reshape(m, D_ATTN)


def _alibi_tables():
    slopes = (LOG2E * np.exp2(-8.0 * np.arange(1, N_HEADS + 1, dtype=np.float64) / N_HEADS)).astype(np.float32)
    c = np.arange(BLOCK, dtype=np.float32)
    past = slopes[:, None, None] * np.broadcast_to(c[:, None], (BLOCK, BLOCK))[None]
    causal = np.where(c[:, None] <= c[None, :], 0.0, NEG).astype(np.float32)
    acol = past[:, :, :LANES].astype(np.float32)
    adiag = (past + causal[None]).astype(np.float32)
    srow = np.broadcast_to(slopes[:, None, None], (N_HEADS, 1, BLOCK)).astype(np.float32)
    return jnp.asarray(acol), jnp.asarray(adiag), jnp.asarray(srow)


def _ssd_chunk(p_ref, cw_ref, cb_ref, dtb_ref, alog_ref, dsk_ref, nw_ref, tri_ref, ex_ref,
               o_ref, xpad_scr, st_scr, y_scr, *, d_ssd, fillers=()):
    fillers = list(fillers)
    d_conv = d_ssd + 2 * SSD_GROUPS * SSD_STATE
    gw = d_ssd // SSD_GROUPS

    u = p_ref[:, d_ssd:d_ssd + d_conv]
    dtr = p_ref[:, d_ssd + d_conv:d_ssd + d_conv + LANES]

    xpad_scr[8:8 + BLOCK, :] = u
    conv = cb_ref[...] + cw_ref[CONV_W - 1:CONV_W, :] * u
    for s in range(1, CONV_W):
        conv = conv + cw_ref[CONV_W - 1 - s:CONV_W - s, :] * xpad_scr[8 - s:8 - s + BLOCK, :]
    xpad_scr[0:8, :] = u[BLOCK - 8:BLOCK, :]
    act = conv * _sigmoid(conv)
    xs = act[:, 0:d_ssd]
    bm = act[:, d_ssd:d_ssd + SSD_GROUPS * SSD_STATE]
    cm = act[:, d_ssd + SSD_GROUPS * SSD_STATE:d_conv]

    v = dtr + dtb_ref[...]
    dt = jnp.maximum(v, 0.0) + jnp.log1p(jnp.exp(-jnp.abs(v)))
    da = dt * (-jnp.exp(alog_ref[...]))
    hi = da.astype(BF16)
    r1 = da - hi.astype(F32)
    mid = r1.astype(BF16)
    lo = (r1 - mid.astype(F32)).astype(BF16)
    tri = tri_ref[...]
    acs = (_dot(tri, hi) + _dot(tri, mid) + _dot(tri, lo)) * LOG2E
    acs_t = acs.T
    e1 = jnp.exp2(acs)
    e2 = jnp.exp2(acs[BLOCK - 1:BLOCK, :] - acs)

    def expand(a):
        a_hi, a_lo = _split2(a)
        return _dot(jnp.concatenate([a_hi, a_lo], axis=1), ex_ref[...])

    dtx = expand(dt)
    e1x = expand(e1)
    w2x = expand(dt * e2)
    xdt_b = (xs * dtx).astype(BF16)

    row = lax.broadcasted_iota(jnp.int32, (BLOCK, BLOCK), 0)
    col = lax.broadcasted_iota(jnp.int32, (BLOCK, BLOCK), 1)
    causal = row >= col
    lane = lax.broadcasted_iota(jnp.int32, (BLOCK, LANES), 1)
    heads_per_group = gw // SSD_HEAD
    for g in range(SSD_GROUPS):
        bg = bm[:, g * SSD_STATE:(g + 1) * SSD_STATE]
        cg_b = cm[:, g * SSD_STATE:(g + 1) * SSD_STATE].astype(BF16)
        gmat = _dot_nt(cg_b, bg.astype(BF16))
        for pr in range(heads_per_group // 2):
            slab = slice(g * gw + pr * LANES, g * gw + (pr + 1) * LANES)
            xp = xdt_b[:, slab]
            pair = None
            for hh in range(2):
                h = g * heads_per_group + pr * 2 + hh
                diff = acs[:, h:h + 1] - acs_t[h:h + 1, :]
                w = (gmat * jnp.exp2(jnp.where(causal, diff, -jnp.inf))).astype(BF16)
                keep = (lane < SSD_HEAD) if hh == 0 else (lane >= SSD_HEAD)
                part = _dot(w, jnp.where(keep, xp, jnp.zeros_like(xp)))
                pair = part if pair is None else pair + part
            y_scr[:, slab] = pair
            if fillers:
                fillers.pop(0)()
        gsl = slice(g * gw, (g + 1) * gw)
        st = st_scr[g]
        y_scr[:, gsl] = y_scr[:, gsl] + _dot(cg_b, st.astype(BF16)) * e1x[:, gsl]
        xw = (xs[:, gsl] * w2x[:, gsl]).astype(BF16)
        st_scr[g] = st * e1x[BLOCK - 1:BLOCK, gsl] + _dot(bg.T.astype(BF16), xw)

    y = y_scr[...] + xs * dsk_ref[...]
    z = p_ref[:, 0:d_ssd]
    y = y * (z * _sigmoid(z))
    for g in range(SSD_GROUPS):
        gsl = slice(g * gw, (g + 1) * gw)
        yg = y[:, gsl]
        ms = jnp.mean(yg * yg, axis=-1, keepdims=True)
        o_ref[:, gsl] = (yg * lax.rsqrt(ms + EPS) * nw_ref[:, gsl]).astype(BF16)
    for f in fillers:
        f()


def _ssd_out_kernel(x_ref, lnw_ref, w_ref, cw_ref, cb_ref, dtb_ref, alog_ref, dsk_ref, nw_ref, tri_ref, ex_ref,
                    attn_ref, wout_ref, o_ref, proj_scr, yb_scr, xpad_scr, st_scr, y_scr, *, d_ssd, cps):
    c = pl.program_id(1)
    d_conv = d_ssd + 2 * SSD_GROUPS * SSD_STATE

    @pl.when(c == 0)
    def _():
        xpad_scr[0:8, :] = jnp.zeros((8, d_conv), F32)
        st_scr[...] = jnp.zeros_like(st_scr)

    x = x_ref[...]
    ms = jnp.mean(x * x, axis=-1, keepdims=True)
    h = (x * lax.rsqrt(ms + EPS) * lnw_ref[...]).astype(BF16)
    n = w_ref.shape[1]
    step = 2 * MXU_COLS
    for lo in range(d_ssd, n, step):
        sl = slice(lo, min(lo + step, n))
        proj_scr[:, sl] = _dot(h, w_ref[:, sl])
    da = attn_ref.shape[1]
    d = o_ref.shape[1]

    def z_piece(sl):
        proj_scr[:, sl] = _dot(h, w_ref[:, sl])

    def attn_out_piece(sl):
        o_ref[:, sl] = x_ref[:, sl] + _dot(attn_ref[...], wout_ref[0:da, sl])

    fillers = [functools.partial(z_piece, slice(lo, lo + MXU_COLS)) for lo in range(0, d_ssd, MXU_COLS)]
    fillers += [functools.partial(attn_out_piece, slice(lo, lo + MXU_COLS)) for lo in range(0, d, MXU_COLS)]
    n_first = len(fillers) if cps == 1 else d_ssd // MXU_COLS + 2
    assert d_ssd // MXU_COLS <= d_ssd // (2 * SSD_HEAD)
    for ck in range(cps):
        rows = pl.ds(ck * BLOCK, BLOCK)
        mine = fillers[:n_first] if ck == 0 else (fillers[n_first:] if ck == cps - 1 else [])
        _ssd_chunk(proj_scr.at[rows], cw_ref, cb_ref, dtb_ref, alog_ref, dsk_ref, nw_ref, tri_ref, ex_ref,
                   yb_scr.at[rows], xpad_scr, st_scr, y_scr, d_ssd=d_ssd, fillers=mine)
    o_ref[...] += _dot(yb_scr[...], wout_ref[da:, :])


def _ssd_out(x2, ln_w, w_ssd, cw, cb, dtb, alog, dsk, nw, tri, ex, attn, w_out, *, batch, seq, d_ssd):
    nc = seq // BLOCK
    cps = 2 if nc % 2 == 0 else 1
    rows = cps * BLOCK
    steps = nc // cps
    m, d = x2.shape
    n = w_ssd.shape[1]
    d_conv = d_ssd + 2 * SSD_GROUPS * SSD_STATE
    gw = d_ssd // SSD_GROUPS
    const = lambda b, c: (0, 0)
    tile = lambda b, c: (b * steps + c, 0)
    return pl.pallas_call(
        functools.partial(_ssd_out_kernel, d_ssd=d_ssd, cps=cps),
        grid=(batch, steps),
        in_specs=[
            pl.BlockSpec((rows, d), tile),
            pl.BlockSpec((1, d), const),
            pl.BlockSpec((d, n), const),
            pl.BlockSpec((CONV_W, d_conv), const),
            pl.BlockSpec((1, d_conv), const),
            pl.BlockSpec((1, LANES), const),
            pl.BlockSpec((1, LANES), const),
            pl.BlockSpec((1, d_ssd), const),
            pl.BlockSpec((1, d_ssd), const),
            pl.BlockSpec((BLOCK, BLOCK), const),
            pl.BlockSpec((2 * LANES, d_ssd), const),
            pl.BlockSpec((rows, attn.shape[1]), tile),
            pl.BlockSpec(w_out.shape, const),
        ],
        out_specs=pl.BlockSpec((rows, d), tile),
        out_shape=jax.ShapeDtypeStruct((m, d), F32),
        scratch_shapes=[
            pltpu.VMEM((rows, n), F32),
            pltpu.VMEM((rows, d_ssd), BF16),
            pltpu.VMEM((8 + BLOCK, d_conv), F32),
            pltpu.VMEM((SSD_GROUPS, SSD_STATE, gw), F32),
            pltpu.VMEM((BLOCK, d_ssd), F32),
        ],
        compiler_params=pltpu.CompilerParams(
            dimension_semantics=("arbitrary", "arbitrary"), vmem_limit_bytes=VMEM_LIMIT),
        name="ssd_out",
    )(x2, ln_w, w_ssd, cw, cb, dtb, alog, dsk, nw, tri, ex, attn, w_out)


def _row_tile(m, want):
    t = want
    while m % t:
        t //= 2
    return t


def _layer(x2, ln_w, w_in, q_norm_w, k_norm_w, conv_w, conv_b, dt_bias, a_log, d_skip, ssd_norm_w, w_out,
           tables, *, batch, seq):
    d = x2.shape[1]
    d_ssd = w_out.shape[0] - D_ATTN
    d_conv = d_ssd + 2 * SSD_GROUPS * SSD_STATE
    alibi, tri, ex = tables
    m = x2.shape[0]

    wq, wk, wv, wg = (w_in[:, k * D_ATTN:(k + 1) * D_ATTN] for k in range(4))
    w_kg = jnp.concatenate([wk, wg], axis=1).astype(BF16)
    w_t = jnp.stack([wq.T, wv.T]).astype(BF16)
    qwb = jnp.broadcast_to((q_norm_w * (HEAD_DIM ** -0.5 * LOG2E))[:, None], (HEAD_DIM, BLOCK))
    w_rest = w_in[:, 4 * D_ATTN:]
    pad = (-w_rest.shape[1]) % LANES
    w_ssd = jnp.pad(w_rest, ((0, 0), (0, pad))).astype(BF16)
    lnw = ln_w.reshape(1, d)

    def lane_pad(v):
        return jnp.pad(v, (0, LANES - v.shape[0])).reshape(1, LANES)

    kg, qt, vt, kmean = _inproj_attn(x2, lnw, w_kg, w_t, k_norm_w.reshape(1, HEAD_DIM), qwb,
                                     tm=_row_tile(m, 1024))
    attn = _moba_attn(kg, qt, vt, kmean, alibi, batch=batch, seq=seq, hg=ATTN_HEADS_PER_STEP)
    return _ssd_out(x2, lnw, w_ssd, conv_w, conv_b.reshape(1, d_conv), lane_pad(dt_bias), lane_pad(a_log),
                    jnp.repeat(d_skip, SSD_HEAD).reshape(1, d_ssd), ssd_norm_w.reshape(1, d_ssd), tri, ex,
                    attn, w_out.astype(BF16), batch=batch, seq=seq, d_ssd=d_ssd)


def kernel(x, ln_w, w_in, q_norm_w, k_norm_w, conv_w, conv_b, dt_bias, a_log, d_skip, ssd_norm_w, w_out):
    batch, seq, d = x.shape
    assert seq % BLOCK == 0 and d == D_ATTN
    d_ssd = w_out.shape[1] - D_ATTN
    n_heads = d_ssd // SSD_HEAD
    alibi = _alibi_tables()
    tri = jnp.asarray(np.tril(np.ones((BLOCK, BLOCK), np.float32))).astype(BF16)
    ex_np = np.zeros((LANES, d_ssd), np.float32)
    ex_np[np.repeat(np.arange(n_heads), SSD_HEAD), np.arange(d_ssd)] = 1.0
    ex = jnp.asarray(np.concatenate([ex_np, ex_np], axis=0)).astype(BF16)
    tables = (alibi, tri, ex)
    h = x.reshape(batch * seq, d)
    for i in range(ln_w.shape[0]):
        h = _layer(h, ln_w[i], w_in[i], q_norm_w[i], k_norm_w[i], conv_w[i], conv_b[i], dt_bias[i],
                   a_log[i], d_skip[i], ssd_norm_w[i], w_out[i], tables, batch=batch, seq=seq)
    return h.reshape(batch, seq, d)
```
